```python
import math
import jax, jax.numpy as jnp
from jax import lax
import numpy as np

D_MODEL = 4096
BATCH = 4
SEQ = 2048
DEPTH = 2
DEC_BATCH = 128
DEC_SEQ = 8
PAST_LEN = 16384
PAGE_SIZE = 128

N_AB_LAYERS = (DEPTH + 1) // 2
N_MLA_LAYERS = DEPTH // 2
CHUNK = 128
A_GROUPS = 8
A_WIDTH = D_MODEL // 2
A_GROUP_DIM = A_WIDTH // A_GROUPS
B_WIDTH = D_MODEL // 2
CONV_W = 3
AB_IN = 2 * A_WIDTH + 3 * B_WIDTH
N_HEADS = 32
QK_NOPE = 128
QK_ROPE = 64
V_HEAD = 128
Q_LORA = 1024
KV_LORA = 512
C_IN = Q_LORA + KV_LORA + QK_ROPE
ROPE_THETA = 10000.0
ATTN_SCALE = 1.0 / math.sqrt(QK_NOPE + QK_ROPE)
Q_BLOCK = 128
D_FF = ((8 * D_MODEL + 3 * 256 - 1) // (3 * 256)) * 256
N_MOD = 6
EPS = 1e-6

kernel_name = "hybrid_gmlp_shortconv_mla_adaln_decode_step"


def rms_norm(x, g):
    xf = x.astype(jnp.float32)
    y = xf * lax.rsqrt(jnp.mean(xf * xf, axis=-1, keepdims=True) + EPS)
    return (y * g.astype(jnp.float32)).astype(x.dtype)


def layer_norm(x, g):
    xf = x.astype(jnp.float32)
    mu = jnp.mean(xf, axis=-1, keepdims=True)
    xc = xf - mu
    y = xc * lax.rsqrt(jnp.mean(xc * xc, axis=-1, keepdims=True) + EPS)
    return (y * g.astype(jnp.float32)).astype(x.dtype)


def adaln(c, w_mod, b_mod):
    m = jnp.einsum('bd,de->be', jax.nn.silu(c), w_mod) + b_mod
    return jnp.split(m, N_MOD, axis=-1)


def modulate(h, shift, scale):
    return h * (1 + scale[:, None, :]) + shift[:, None, :]


def rope(x, pos):
    half = x.shape[-1] // 2
    inv = ROPE_THETA ** (-jnp.arange(half, dtype=jnp.float32) / half)
    ang = pos.astype(jnp.float32)[:, None] * inv[None, :]
    ang = ang.reshape(ang.shape[:1] + (1,) * (x.ndim - 3) + (half,))
    cos, sin = jnp.cos(ang), jnp.sin(ang)
    xf = x.astype(jnp.float32)
    x1, x2 = xf[..., :half], xf[..., half:]
    return jnp.concatenate([x1 * cos - x2 * sin, x2 * cos + x1 * sin], axis=-1).astype(x.dtype)


def spatial_gate(u, v, w_s, b_s):
    L = v.shape[-3]
    w = jnp.where(jnp.tril(jnp.ones((L, L), bool)), w_s[:, :L, :L], 0)
    s = jnp.einsum('gij,...jgc->...igc', w, v) + b_s[:, :L].T[:, :, None]
    return u * s


def short_conv(x, buf, w):
    s = x.shape[1]
    xp = jnp.concatenate([buf, x], axis=1)
    y = sum(xp[:, k:k + s] * w[k] for k in range(CONV_W))
    return y, xp[:, s:]


def ab_mixer(h, conv_buf, is_prompt, w_in, w_out, w_s, b_s, g_v, conv_w):
    bsz, s = h.shape[:2]
    z = jnp.einsum('bsd,de->bse', h, w_in)
    a2 = 2 * A_WIDTH
    u = jax.nn.gelu(z[..., :A_WIDTH])
    v = layer_norm(jax.nn.gelu(z[..., A_WIDTH:a2]), g_v)
    gate_b = z[..., a2:a2 + B_WIDTH]
    gate_c = z[..., a2 + B_WIDTH:a2 + 2 * B_WIDTH]
    h_in = z[..., a2 + 2 * B_WIDTH:]
    if is_prompt:
        n = s // CHUNK
        shp = (bsz, n, CHUNK, A_GROUPS, A_GROUP_DIM)
    else:
        shp = (bsz, s, A_GROUPS, A_GROUP_DIM)
    ya = spatial_gate(u.reshape(shp), v.reshape(shp), w_s, b_s).reshape(bsz, s, A_WIDTH)
    yc, new_buf = short_conv(gate_c * h_in, conv_buf, conv_w)
    yb = gate_b * yc
    out = jnp.einsum('bsf,fd->bsd', jnp.concatenate([ya, yb], axis=-1), w_out)
    return out, new_buf, v


def mla_project(h, pos, w_in, g_q, g_kv, w_q_up, w_kv_up):
    bsz, s = h.shape[:2]
    z = jnp.einsum('bsd,de->bse', h, w_in)
    cq = rms_norm(z[..., :Q_LORA], g_q)
    ckv = rms_norm(z[..., Q_LORA:Q_LORA + KV_LORA], g_kv)
    kr = rope(z[..., Q_LORA + KV_LORA:], pos)
    q = jnp.einsum('bsq,qe->bse', cq, w_q_up).reshape(bsz, s, N_HEADS, QK_NOPE + QK_ROPE)
    q_rope = rope(q[..., QK_NOPE:], pos)
    w_uk = w_kv_up.reshape(KV_LORA, N_HEADS, QK_NOPE + V_HEAD)[..., :QK_NOPE]
    q_lat = jnp.einsum('bshn,chn->bshc', q[..., :QK_NOPE], w_uk)
    return q_lat, q_rope, ckv, kr


def mla_attend(q_lat, q_rope, ckv, kr, q_pos, k_pos):
    s = (jnp.einsum('...qhc,...kc->...hqk', q_lat, ckv).astype(jnp.float32)
         + jnp.einsum('...qhr,...kr->...hqk', q_rope, kr).astype(jnp.float32)) * ATTN_SCALE
    s = jnp.where(k_pos[None, :] <= q_pos[:, None], s, -jnp.inf)
    p = jax.nn.softmax(s, axis=-1).astype(ckv.dtype)
    return jnp.einsum('...hqk,...kc->...qhc', p, ckv)


def mla_output(o_lat, w_kv_up, w_out):
    w_uv = w_kv_up.reshape(KV_LORA, N_HEADS, QK_NOPE + V_HEAD)[..., QK_NOPE:]
    o = jnp.einsum('bshc,chv->bshv', o_lat, w_uv)
    o = o.reshape(o.shape[:2] + (N_HEADS * V_HEAD,))
    return jnp.einsum('bsf,fd->bsd', o, w_out)


def mla_prompt_attend(q_lat, q_rope, ckv, kr, pos):
    bsz, s = q_lat.shape[:2]
    nb = s // Q_BLOCK
    ql = q_lat.reshape(bsz, nb, Q_BLOCK, N_HEADS, KV_LORA).swapaxes(0, 1)
    qr = q_rope.reshape(bsz, nb, Q_BLOCK, N_HEADS, QK_ROPE).swapaxes(0, 1)

    def block(args):
        a, b, qp = args
        return mla_attend(a, b, ckv, kr, qp, pos)

    o = lax.map(block, (ql, qr, pos.reshape(nb, Q_BLOCK)))
    return o.swapaxes(0, 1).reshape(bsz, s, N_HEADS, KV_LORA)


def mla_sample_attend(q_lat, q_rope, ckv, kr, pos, cache_kv, cache_kr, layer, page_table):
    past = page_table.shape[1] * PAGE_SIZE
    k_pos = jnp.arange(past + pos.shape[0], dtype=jnp.int32)

    def one_seq(args):
        ql, qr, cn, kn, pt = args
        ck = jnp.concatenate([cache_kv[layer, pt].reshape(past, KV_LORA), cn], axis=0)
        kk = jnp.concatenate([cache_kr[layer, pt].reshape(past, QK_ROPE), kn], axis=0)
        return mla_attend(ql, qr, ck, kk, pos, k_pos)

    return lax.map(one_seq, (q_lat, q_rope, ckv, kr, page_table))


def swiglu(h, w_gate, w_up, w_down):
    g = jnp.einsum('bsd,df->bsf', h, w_gate)
    u = jnp.einsum('bsd,df->bsf', h, w_up)
    return jnp.einsum('bsf,fd->bsd', jax.nn.silu(g) * u, w_down)


def setup_inputs(seed: int = 0) -> dict:
    key = jax.random.key(seed)
    ks = jax.random.split(key, 32)
    f32 = jnp.float32

    def nrm(k, shape, scale):
        return jax.random.normal(k, shape, f32) * scale

    def gain(k, shape):
        return 1.0 + 0.02 * jax.random.normal(k, shape, f32)

    n_pages = PAST_LEN // PAGE_SIZE
    n_used = DEC_BATCH * n_pages
    n_pool = n_used + n_used // 4
    page_table = jax.random.permutation(ks[0], n_pool)[:n_used].reshape(DEC_BATCH, n_pages).astype(jnp.int32)
    return {
        'x_prompt': nrm(ks[1], (BATCH, SEQ, D_MODEL), 1.0),
        'x_sample': nrm(ks[2], (DEC_BATCH, DEC_SEQ, D_MODEL), 1.0),
        'state_conv': nrm(ks[3], (N_AB_LAYERS, DEC_BATCH, CONV_W - 1, B_WIDTH), 1.0),
        'cache_kv_latent': nrm(ks[4], (N_MLA_LAYERS, n_pool, PAGE_SIZE, KV_LORA), 1.0),
        'cache_k_rope': nrm(ks[5], (N_MLA_LAYERS, n_pool, PAGE_SIZE, QK_ROPE), 1.0),
        'page_table': page_table,
        'c_prompt': nrm(ks[6], (BATCH, D_MODEL), 1.0),
        'c_sample': nrm(ks[7], (DEC_BATCH, D_MODEL), 1.0),
        'norm_g': gain(ks[8], (DEPTH, 2, D_MODEL)),
        'w_mod': nrm(ks[9], (DEPTH, D_MODEL, N_MOD * D_MODEL), 0.5 * D_MODEL ** -0.5),
        'b_mod': nrm(ks[10], (DEPTH, N_MOD * D_MODEL), 0.02),
        'w_ab_in': nrm(ks[11], (N_AB_LAYERS, D_MODEL, AB_IN), D_MODEL ** -0.5),
        'w_ab_out': nrm(ks[12], (N_AB_LAYERS, A_WIDTH + B_WIDTH, D_MODEL), (A_WIDTH + B_WIDTH) ** -0.5),
        'a_spatial_w': nrm(ks[13], (N_AB_LAYERS, A_GROUPS, CHUNK, CHUNK), CHUNK ** -0.5),
        'a_spatial_b': gain(ks[14], (N_AB_LAYERS, A_GROUPS, CHUNK)),
        'a_v_norm_g': gain(ks[15], (N_AB_LAYERS, A_WIDTH)),
        'b_conv_w': nrm(ks[16], (N_AB_LAYERS, CONV_W, B_WIDTH), CONV_W ** -0.5),
        'c_w_in': nrm(ks[17], (N_MLA_LAYERS, D_MODEL, C_IN), D_MODEL ** -0.5),
        'c_q_norm_g': gain(ks[18], (N_MLA_LAYERS, Q_LORA)),
        'c_kv_norm_g': gain(ks[19], (N_MLA_LAYERS, KV_LORA)),
        'c_w_q_up': nrm(ks[20], (N_MLA_LAYERS, Q_LORA, N_HEADS * (QK_NOPE + QK_ROPE)), Q_LORA ** -0.5),
        'c_w_kv_up': nrm(ks[21], (N_MLA_LAYERS, KV_LORA, N_HEADS * (QK_NOPE + V_HEAD)), KV_LORA ** -0.5),
        'c_w_out': nrm(ks[22], (N_MLA_LAYERS, N_HEADS * V_HEAD, D_MODEL), (N_HEADS * V_HEAD) ** -0.5),
        'ffn_w_gate': nrm(ks[23], (DEPTH, D_MODEL, D_FF), D_MODEL ** -0.5),
        'ffn_w_up': nrm(ks[24], (DEPTH, D_MODEL, D_FF), D_MODEL ** -0.5),
        'ffn_w_down': nrm(ks[25], (DEPTH, D_FF, D_MODEL), D_FF ** -0.5),
        'final_norm_g': gain(ks[26], (D_MODEL,)),
    }


def reference(x_prompt, x_sample, state_conv, cache_kv_latent, cache_k_rope, page_table, c_prompt, c_sample,
              norm_g, w_mod, b_mod, w_ab_in, w_ab_out, a_spatial_w, a_spatial_b, a_v_norm_g, b_conv_w,
              c_w_in, c_q_norm_g, c_kv_norm_g, c_w_q_up, c_w_kv_up, c_w_out,
              ffn_w_gate, ffn_w_up, ffn_w_down, final_norm_g):
    pos_p = jnp.arange(x_prompt.shape[1], dtype=jnp.int32)
    pos_s = page_table.shape[1] * PAGE_SIZE + jnp.arange(x_sample.shape[1], dtype=jnp.int32)
    xp, xs = x_prompt, x_sample
    conv_p, conv_s, chunk_v_s = [], [], []
    kv_p, kr_p, kv_s, kr_s = [], [], [], []
    for li in range(DEPTH):
        mp = adaln(c_prompt, w_mod[li], b_mod[li])
        ms = adaln(c_sample, w_mod[li], b_mod[li])
        hp = modulate(rms_norm(xp, norm_g[li, 0]), mp[0], mp[1])
        hs = modulate(rms_norm(xs, norm_g[li, 0]), ms[0], ms[1])
        if li % 2 == 0:
            a = li // 2
            ab_w = (w_ab_in[a], w_ab_out[a], a_spatial_w[a], a_spatial_b[a], a_v_norm_g[a], b_conv_w[a])
            zero_buf = jnp.zeros((xp.shape[0], CONV_W - 1, B_WIDTH), xp.dtype)
            op, buf_p, _ = ab_mixer(hp, zero_buf, True, *ab_w)
            osm, buf_s, v_s = ab_mixer(hs, state_conv[a], False, *ab_w)
            conv_p.append(buf_p)
            conv_s.append(buf_s)
            chunk_v_s.append(v_s)
        else:
            m = li // 2
            mla_w = (c_w_in[m], c_q_norm_g[m], c_kv_norm_g[m], c_w_q_up[m], c_w_kv_up[m])
            ql, qr, ckv, kr = mla_project(hp, pos_p, *mla_w)
            op = mla_output(mla_prompt_attend(ql, qr, ckv, kr, pos_p), c_w_kv_up[m], c_w_out[m])
            kv_p.append(ckv)
            kr_p.append(kr)
            ql, qr, ckv, kr = mla_project(hs, pos_s, *mla_w)
            o_lat = mla_sample_attend(ql, qr, ckv, kr, pos_s, cache_kv_latent, cache_k_rope, m, page_table)
            osm = mla_output(o_lat, c_w_kv_up[m], c_w_out[m])
            kv_s.append(ckv)
            kr_s.append(kr)
        xp = xp + mp[2][:, None, :] * op
        xs = xs + ms[2][:, None, :] * osm
        hp = modulate(rms_norm(xp, norm_g[li, 1]), mp[3], mp[4])
        hs = modulate(rms_norm(xs, norm_g[li, 1]), ms[3], ms[4])
        xp = xp + mp[5][:, None, :] * swiglu(hp, ffn_w_gate[li], ffn_w_up[li], ffn_w_down[li])
        xs = xs + ms[5][:, None, :] * swiglu(hs, ffn_w_gate[li], ffn_w_up[li], ffn_w_down[li])
    y_prompt = rms_norm(xp, final_norm_g)
    y_sample = rms_norm(xs, final_norm_g)
    return (y_prompt, y_sample, jnp.stack(conv_p), jnp.stack(conv_s), jnp.stack(chunk_v_s),
            jnp.stack(kv_p), jnp.stack(kr_p), jnp.stack(kv_s), jnp.stack(kr_s))
```

```python
import functools
import math

import jax
import jax.numpy as jnp
from jax import lax
from jax.experimental import pallas as pl
from jax.experimental.pallas import tpu as pltpu

F32 = jnp.float32
BF16 = jnp.bfloat16

D_MODEL = 4096
BATCH = 4
SEQ = 2048
DEC_BATCH = 128
DEC_SEQ = 8
PAST_LEN = 16384
PAGE_SIZE = 128
N_PAGES = PAST_LEN // PAGE_SIZE
CHUNK = 128
A_GROUPS = 8
A_WIDTH = D_MODEL // 2
A_GROUP_DIM = A_WIDTH // A_GROUPS
B_WIDTH = D_MODEL // 2
AB_IN = 2 * A_WIDTH + 3 * B_WIDTH
N_HEADS = 32
QK_NOPE = 128
QK_ROPE = 64
V_HEAD = 128
Q_LORA = 1024
KV_LORA = 512
ROPE_THETA = 10000.0
ATTN_SCALE = 1.0 / math.sqrt(QK_NOPE + QK_ROPE)
D_FF = 11008
N_MOD = 6
EPS = 1e-6

TP = BATCH * SEQ
TS = DEC_BATCH * DEC_SEQ
T = TP + TS
N_SEQ = BATCH + DEC_BATCH
N_SEQ_PAD = 136

MIB = 1024 * 1024
W_CONVERT_ROWS = 512

ARB = "arbitrary"


def _params(n_axes, vmem_mib):
    return pltpu.CompilerParams(dimension_semantics=(ARB,) * n_axes,
                                vmem_limit_bytes=vmem_mib * MIB)


def _dot(a, b):
    return jnp.dot(a, b, preferred_element_type=F32)


def _dot_nt(a, b):
    return lax.dot_general(a, b, (((1,), (1,)), ((), ())), preferred_element_type=F32)


ADALN_TN = 512


def _adaln_body(c_ref, w_ref, b_ref, o_ref):
    c = c_ref[...]
    a = (c * jax.nn.sigmoid(c)).astype(BF16)
    w = w_ref[...].astype(BF16)
    o_ref[...] = _dot(a, w) + b_ref[...]


def _adaln(c_all, w_mod, b_mod):
    depth = w_mod.shape[0]
    n = w_mod.shape[2]
    return pl.pallas_call(
        _adaln_body,
        grid=(depth, n // ADALN_TN),
        in_specs=[
            pl.BlockSpec((N_SEQ_PAD, D_MODEL), lambda l, j: (0, 0)),
            pl.BlockSpec((None, D_MODEL, ADALN_TN), lambda l, j: (l, 0, j)),
            pl.BlockSpec((None, 1, ADALN_TN), lambda l, j: (l, 0, j)),
        ],
        out_specs=pl.BlockSpec((None, N_SEQ_PAD, ADALN_TN), lambda l, j: (l, 0, j)),
        out_shape=jax.ShapeDtypeStruct((depth, N_SEQ_PAD, n), F32),
        compiler_params=_params(2, 40),
        name="adaln",
    )(c_all, w_mod, b_mod.reshape(depth, 1, n))


def _mod_specs(tm, tn, col_blk, ij):
    n_p_tiles = TP // tm
    tiles_per_seq = SEQ // tm

    def p_map(*g):
        i, j = ij(*g)
        return (jnp.minimum(i // tiles_per_seq, BATCH - 1), 0, col_blk(j))

    def s_map(*g):
        i, j = ij(*g)
        return (jnp.maximum(i - n_p_tiles, 0), col_blk(j))

    return [pl.BlockSpec((None, 1, tn), p_map), pl.BlockSpec((tm, tn), s_map)]


def _pick_mod(i, tm, p_ref, s_ref):
    return jnp.where(i < TP // tm, p_ref[...], s_ref[...])


NORM_TM = 256


def _rms(x, g):
    return x * lax.rsqrt(jnp.mean(x * x, axis=-1, keepdims=True) + EPS) * g


def _norm_mod_body(x_ref, g_ref, shp_ref, shs_ref, scp_ref, scs_ref, o_ref):
    i = pl.program_id(0)
    y = _rms(x_ref[...], g_ref[...])
    shift = _pick_mod(i, NORM_TM, shp_ref, shs_ref)
    scale = _pick_mod(i, NORM_TM, scp_ref, scs_ref)
    o_ref[...] = (y * (1.0 + scale) + shift).astype(BF16)


def _norm_mod(x, g_all, g_idx, mod_p, mod_s, shift_chunk):
    ij = lambda i: (i, 0)
    specs = (_mod_specs(NORM_TM, D_MODEL, lambda j: shift_chunk, ij)
             + _mod_specs(NORM_TM, D_MODEL, lambda j: shift_chunk + 1, ij))
    return pl.pallas_call(
        _norm_mod_body,
        grid=(T // NORM_TM,),
        in_specs=[pl.BlockSpec((NORM_TM, D_MODEL), lambda i: (i, 0)),
                  pl.BlockSpec((None, 1, D_MODEL), lambda i: (g_idx, 0, 0))] + specs,
        out_specs=pl.BlockSpec((NORM_TM, D_MODEL), lambda i: (i, 0)),
        out_shape=jax.ShapeDtypeStruct((T, D_MODEL), BF16),
        compiler_params=_params(1, 48),
        name="norm_mod",
    )(x, g_all, mod_p, mod_s, mod_p, mod_s)


def _final_norm_body(x_ref, g_ref, o_ref):
    o_ref[...] = _rms(x_ref[...], g_ref[...])


def _final_norm(x, g, row_blk0, n_rows, out_shape3):
    s = out_shape3[1]
    if s >= NORM_TM:
        per = s // NORM_TM
        o_spec = pl.BlockSpec((None, NORM_TM, D_MODEL), lambda i: (i // per, i % per, 0))
        out = jax.ShapeDtypeStruct(out_shape3, F32)
    else:
        o_spec = pl.BlockSpec((NORM_TM, D_MODEL), lambda i: (i, 0))
        out = jax.ShapeDtypeStruct((n_rows, D_MODEL), F32)
    y = pl.pallas_call(
        _final_norm_body,
        grid=(n_rows // NORM_TM,),
        in_specs=[pl.BlockSpec((NORM_TM, D_MODEL), lambda i: (i + row_blk0, 0)),
                  pl.BlockSpec((1, D_MODEL), lambda i: (0, 0))],
        out_specs=o_spec,
        out_shape=out,
        compiler_params=_params(1, 32),
        name="final_norm",
    )(x, g.reshape(1, D_MODEL))
    return y.reshape(out_shape3)


def _mm(a, ws, epilogue, *, tm, tn, n_cols, out_shapes, out_specs, name, vmem_mib,
        w_layer=0, w_col_off=None, tk=None, kblk=0, extras=(), extra_specs=()):
    m = a.shape[0]
    tk = a.shape[1] if tk is None else tk
    nw, ne, no = len(ws), len(extras), len(out_shapes)
    w_col_off = (0,) * nw if w_col_off is None else w_col_off

    def body(*refs):
        a_ref = refs[0]
        w_refs = refs[1:1 + nw]
        ex = refs[1 + nw:1 + nw + ne]
        outs = refs[1 + nw + ne:1 + nw + ne + no]
        wbf = refs[1 + nw + ne + no:]

        @pl.when(pl.program_id(1) == 0)
        def _():
            for w, wb in zip(w_refs, wbf):
                for r in range(0, tk, W_CONVERT_ROWS):
                    rows = min(W_CONVERT_ROWS, tk - r)
                    wb[r:r + rows, :] = w[r:r + rows, :].astype(BF16)

        av = a_ref[...]
        accs = [_dot(av, wb[...]) for wb in wbf]
        epilogue(accs, ex, outs)

    in_specs = [pl.BlockSpec((tm, tk), lambda j, i: (i, kblk))]
    for off in w_col_off:
        in_specs.append(pl.BlockSpec((None, tk, tn), lambda j, i, off=off: (w_layer, kblk, j + off)))
    in_specs += list(extra_specs)
    return pl.pallas_call(
        body,
        grid=(n_cols // tn, m // tm),
        in_specs=in_specs,
        out_specs=out_specs,
        out_shape=out_shapes,
        scratch_shapes=[pltpu.VMEM((tk, tn), BF16) for _ in range(nw)],
        compiler_params=_params(2, vmem_mib),
        name=name,
    )(a, *ws, *extras)


def _tile_spec(tm, tn, col_off=0):
    return pl.BlockSpec((tm, tn), lambda j, i: (i, j + col_off))


def _row_spec(n):
    return lambda tn: pl.BlockSpec((1, tn), lambda j, i: (0, j))


def _residual_mm(a, w, w_layer, x, mod_p, mod_s, gate_chunk, *, tm, tn, name, vmem_mib,
                 tk=None, kblk=0, partial=None):
    ij = lambda j, i: (i, j)
    gate_specs = _mod_specs(tm, tn, lambda j: gate_chunk * (D_MODEL // tn) + j, ij)
    has_partial = partial is not None

    def epilogue(accs, ex, outs):
        i = pl.program_id(1)
        acc = accs[0]
        if has_partial:
            acc = acc + ex[3][...]
        gate = _pick_mod(i, tm, ex[1], ex[2])
        outs[0][...] = ex[0][...] + gate * acc

    extras = [x, mod_p, mod_s] + ([partial] if has_partial else [])
    extra_specs = [_tile_spec(tm, tn)] + gate_specs + ([_tile_spec(tm, tn)] if has_partial else [])
    return _mm(a, [w], epilogue, tm=tm, tn=tn, n_cols=D_MODEL,
               out_shapes=[jax.ShapeDtypeStruct((T, D_MODEL), F32)],
               out_specs=[_tile_spec(tm, tn)], name=name, vmem_mib=vmem_mib,
               w_layer=w_layer, tk=tk, kblk=kblk, extras=extras, extra_specs=extra_specs)[0]


def _partial_mm(a, w, w_layer, *, tm, tn, tk, kblk, name, vmem_mib):
    def epilogue(accs, ex, outs):
        outs[0][...] = accs[0]

    return _mm(a, [w], epilogue, tm=tm, tn=tn, n_cols=D_MODEL,
               out_shapes=[jax.ShapeDtypeStruct((T, D_MODEL), F32)],
               out_specs=[_tile_spec(tm, tn)], name=name, vmem_mib=vmem_mib,
               w_layer=w_layer, tk=tk, kblk=kblk)[0]


MIX_TM = CHUNK


def _layer_norm(x, g):
    mu = jnp.mean(x, axis=-1, keepdims=True)
    xc = x - mu
    return xc * lax.rsqrt(jnp.mean(xc * xc, axis=-1, keepdims=True) + EPS) * g


def _mixer_body(is_prompt, u_ref, vg_ref, gb_ref, gc_ref, hi_ref, h1_ref, h2_ref,
                ws_ref, bs_ref, gv_ref, cw_ref, *rest):
    if is_prompt:
        yab_ref, tail_ref = rest
    else:
        _, yab_ref, tail_ref, v_ref = rest
    i = pl.program_id(0)
    v = _layer_norm(vg_ref[...], gv_ref[...])
    if not is_prompt:
        v_ref[...] = v
    vb = v.astype(BF16)
    row = lax.broadcasted_iota(jnp.int32, (MIX_TM, MIX_TM), 0)
    col = lax.broadcasted_iota(jnp.int32, (MIX_TM, MIX_TM), 1)
    mask = col <= row
    if not is_prompt:
        mask = mask & ((col // DEC_SEQ) == (row // DEC_SEQ))
    for g in range(A_GROUPS):
        sl = slice(g * A_GROUP_DIM, (g + 1) * A_GROUP_DIM)
        w = jnp.where(mask, ws_ref[g], 0.0).astype(BF16)
        s = _dot(w, vb[:, sl]) + bs_ref[g]
        yab_ref[:, sl] = (u_ref[:, sl] * s).astype(BF16)

    gch = gc_ref[...] * hi_ref[...]
    pos = lax.broadcasted_iota(jnp.int32, (MIX_TM, B_WIDTH), 0)
    r1 = pltpu.roll(gch, 1, 0)
    r2 = pltpu.roll(gch, 2, 0)
    if is_prompt:
        halo = jnp.where(i % (SEQ // MIX_TM) == 0, 0.0, h1_ref[...] * h2_ref[...])
        p1 = halo[7:8, :]
        p2 = halo[6:7, :]
        x1 = jnp.where(pos == 0, p1, r1)
        x2 = jnp.where(pos == 0, p2, jnp.where(pos == 1, p1, r2))
    else:
        pos = pos % DEC_SEQ
        e0 = h1_ref[...]
        e1 = h2_ref[...]
        x1 = jnp.where(pos == 0, e1, r1)
        x2 = jnp.where(pos == 0, e0, jnp.where(pos == 1, e1, r2))
    yc = x2 * cw_ref[0:1, :] + x1 * cw_ref[1:2, :] + gch * cw_ref[2:3, :]
    yab_ref[:, A_WIDTH:] = (gb_ref[...] * yc).astype(BF16)
    if is_prompt:
        tail_ref[...] = gch[MIX_TM - 8:, :]
    else:
        tail_ref[...] = gch


def _mixer(z, is_prompt, halo_a, halo_b, ws_eff, bs_eff, g_v, conv_w, yab_in=None):
    n_tiles = (TP if is_prompt else TS) // MIX_TM
    blk0 = 0 if is_prompt else TP // MIX_TM
    zc = lambda c: pl.BlockSpec((MIX_TM, A_WIDTH), lambda i, c=c: (i + blk0, c))
    if is_prompt:
        rows8 = MIX_TM // 8
        halo_specs = [pl.BlockSpec((8, B_WIDTH), lambda i: (jnp.maximum(i * rows8 - 1, 0), 3)),
                      pl.BlockSpec((8, B_WIDTH), lambda i: (jnp.maximum(i * rows8 - 1, 0), 4))]
    else:
        halo_specs = [pl.BlockSpec((MIX_TM, B_WIDTH), lambda i: (i, 0)),
                      pl.BlockSpec((MIX_TM, B_WIDTH), lambda i: (i, 0))]
    in_specs = [zc(0), zc(1), zc(2), zc(3), zc(4)] + halo_specs + [
        pl.BlockSpec((A_GROUPS, MIX_TM, MIX_TM), lambda i: (0, 0, 0)),
        pl.BlockSpec((A_GROUPS, MIX_TM, 1), lambda i: (0, 0, 0)),
        pl.BlockSpec((1, A_WIDTH), lambda i: (0, 0)),
        pl.BlockSpec((3, B_WIDTH), lambda i: (0, 0)),
    ]
    args = [z, z, z, z, z, halo_a, halo_b, ws_eff, bs_eff, g_v, conv_w]
    yab_spec = pl.BlockSpec((MIX_TM, D_MODEL), lambda i: (i + blk0, 0))
    yab_shape = jax.ShapeDtypeStruct((T, D_MODEL), BF16)
    if is_prompt:
        out_shapes = [yab_shape, jax.ShapeDtypeStruct((n_tiles * 8, B_WIDTH), F32)]
        out_specs = [yab_spec, pl.BlockSpec((8, B_WIDTH), lambda i: (i, 0))]
        aliases = {}
    else:
        in_specs.append(pl.BlockSpec(memory_space=pl.ANY))
        args.append(yab_in)
        out_shapes = [yab_shape, jax.ShapeDtypeStruct((TS, B_WIDTH), F32),
                      jax.ShapeDtypeStruct((TS, A_WIDTH), F32)]
        out_specs = [yab_spec, pl.BlockSpec((MIX_TM, B_WIDTH), lambda i: (i, 0)),
                     pl.BlockSpec((MIX_TM, A_WIDTH), lambda i: (i, 0))]
        aliases = {len(args) - 1: 0}
    return pl.pallas_call(
        functools.partial(_mixer_body, is_prompt),
        grid=(n_tiles,),
        in_specs=in_specs,
        out_specs=out_specs,
        out_shape=out_shapes,
        input_output_aliases=aliases,
        compiler_params=_params(1, 40),
        name="mixer_prompt" if is_prompt else "mixer_sample",
    )(*args)


HEAD_TM = 1024


def _head_mm_body(a_ref, w_ref, o_ref, wb_ref):
    @pl.when(pl.program_id(1) == 0)
    def _():
        wb_ref[...] = w_ref[...].astype(BF16)

    o_ref[...] = _dot(a_ref[...], wb_ref[...]).astype(BF16)


def _head_mm(a, w, k_h, n_h, name):
    return pl.pallas_call(
        _head_mm_body,
        grid=(N_HEADS, T // HEAD_TM),
        in_specs=[pl.BlockSpec((HEAD_TM, k_h), lambda h, i: (i, h)),
                  pl.BlockSpec((None, k_h, n_h), lambda h, i: (h, 0, 0))],
        out_specs=pl.BlockSpec((HEAD_TM, n_h), lambda h, i: (i, h)),
        out_shape=jax.ShapeDtypeStruct((T, N_HEADS * n_h), BF16),
        scratch_shapes=[pltpu.VMEM((k_h, n_h), BF16)],
        compiler_params=_params(2, 32),
        name=name,
    )(a, w)


PQ = 128
PK = 256
P_ROWS = PQ * N_HEADS


def _online_softmax_step(s, v, m_ref, l_ref, acc_ref):
    m_prev = m_ref[...]
    m_new = jnp.maximum(m_prev, jnp.max(s, axis=-1, keepdims=True))
    alpha = jnp.exp((m_prev - m_new) * ATTN_SCALE)
    p = jnp.exp((s - m_new) * ATTN_SCALE)
    l_ref[...] = alpha * l_ref[...] + jnp.sum(p, axis=-1, keepdims=True)
    acc_ref[...] = alpha * acc_ref[...] + _dot(p.astype(BF16), v)
    m_ref[...] = m_new


def _prompt_attn_body(ql_ref, qr_ref, kv_ref, kr_ref, o_ref, m_ref, l_ref, acc_ref):
    qb = pl.program_id(1)
    kb = pl.program_id(2)
    last = (qb * PQ + PQ - 1) // PK

    @pl.when(kb == 0)
    def _():
        m_ref[...] = jnp.full_like(m_ref, -jnp.inf)
        l_ref[...] = jnp.zeros_like(l_ref)
        acc_ref[...] = jnp.zeros_like(acc_ref)

    @pl.when(kb <= last)
    def _():
        q = ql_ref[...].reshape(P_ROWS, KV_LORA)
        qr = qr_ref[...].reshape(P_ROWS, QK_ROPE)
        kv = kv_ref[...]
        s = _dot_nt(q, kv) + _dot_nt(qr, kr_ref[...])
        tok = qb * PQ + lax.broadcasted_iota(jnp.int32, (P_ROWS, PK), 0) // N_HEADS
        key = kb * PK + lax.broadcasted_iota(jnp.int32, (P_ROWS, PK), 1)
        s = jnp.where(key <= tok, s, -jnp.inf)
        _online_softmax_step(s, kv, m_ref, l_ref, acc_ref)

    @pl.when(kb == last)
    def _():
        o = acc_ref[...] / l_ref[...]
        o_ref[...] = o.astype(BF16).reshape(PQ, N_HEADS, KV_LORA)


def _prompt_attn(q_lat3, q_rope3, ckv_bf, kr_bf):
    nq = SEQ // PQ
    nk = SEQ // PK

    def k_map(b, qb, kb):
        return (b * nk + jnp.minimum(kb, (qb * PQ + PQ - 1) // PK), 0)

    return pl.pallas_call(
        _prompt_attn_body,
        grid=(BATCH, nq, nk),
        in_specs=[pl.BlockSpec((PQ, N_HEADS, KV_LORA), lambda b, qb, kb: (b * nq + qb, 0, 0)),
                  pl.BlockSpec((PQ, N_HEADS, QK_ROPE), lambda b, qb, kb: (b * nq + qb, 0, 0)),
                  pl.BlockSpec((PK, KV_LORA), k_map),
                  pl.BlockSpec((PK, QK_ROPE), k_map)],
        out_specs=pl.BlockSpec((PQ, N_HEADS, KV_LORA), lambda b, qb, kb: (b * nq + qb, 0, 0)),
        out_shape=jax.ShapeDtypeStruct((T, N_HEADS, KV_LORA), BF16),
        scratch_shapes=[pltpu.VMEM((P_ROWS, 1), F32), pltpu.VMEM((P_ROWS, 1), F32),
                        pltpu.VMEM((P_ROWS, KV_LORA), F32)],
        compiler_params=_params(3, 48),
        name="prompt_attn",
    )(q_lat3, q_rope3, ckv_bf, kr_bf)


S_PAGES = 16
S_KEYS = S_PAGES * PAGE_SIZE
S_CHUNKS = N_PAGES // S_PAGES
S_ROWS = DEC_SEQ * N_HEADS


def _sample_attn_body(pt_ref, ql_ref, qr_ref, *rest):
    kv_pages = rest[:S_PAGES]
    kr_pages = rest[S_PAGES:2 * S_PAGES]
    kvn_ref, krn_ref, _, o_ref, kbuf, rbuf, m_ref, l_ref, acc_ref = rest[2 * S_PAGES:]
    c = pl.program_id(1)

    @pl.when(c == 0)
    def _():
        m_ref[...] = jnp.full_like(m_ref, -jnp.inf)
        l_ref[...] = jnp.zeros_like(l_ref)
        acc_ref[...] = jnp.zeros_like(acc_ref)

    for k in range(S_PAGES):
        kbuf[k * PAGE_SIZE:(k + 1) * PAGE_SIZE, :] = kv_pages[k][...].astype(BF16)
        rbuf[k * PAGE_SIZE:(k + 1) * PAGE_SIZE, :] = kr_pages[k][...].astype(BF16)

    q = ql_ref[...].reshape(S_ROWS, KV_LORA)
    qr = qr_ref[...].reshape(S_ROWS, QK_ROPE)
    kv = kbuf[...]
    s = _dot_nt(q, kv) + _dot_nt(qr, rbuf[...])
    _online_softmax_step(s, kv, m_ref, l_ref, acc_ref)

    @pl.when(c == S_CHUNKS - 1)
    def _():
        pad = PAGE_SIZE - DEC_SEQ
        kvn = jnp.concatenate([kvn_ref[...], jnp.zeros((pad, KV_LORA), F32)], axis=0).astype(BF16)
        krn = jnp.concatenate([krn_ref[...], jnp.zeros((pad, QK_ROPE), F32)], axis=0).astype(BF16)
        sn = _dot_nt(q, kvn) + _dot_nt(qr, krn)
        tok = lax.broadcasted_iota(jnp.int32, (S_ROWS, PAGE_SIZE), 0) // N_HEADS
        key = lax.broadcasted_iota(jnp.int32, (S_ROWS, PAGE_SIZE), 1)
        sn = jnp.where(key <= tok, sn, -jnp.inf)
        _online_softmax_step(sn, kvn, m_ref, l_ref, acc_ref)
        o = acc_ref[...] / l_ref[...]
        o_ref[...] = o.astype(BF16).reshape(DEC_SEQ, N_HEADS, KV_LORA)


def _sample_attn(page_table, q_lat3, q_rope3, cache_kv, cache_kr, layer, ckv_f32, kr_f32, o_in):
    blk0 = TP // DEC_SEQ

    def page_map(k):
        return lambda b, c, pt: (layer, pt[b * N_PAGES + c * S_PAGES + k], 0, 0)

    q_map = lambda b, c, pt: (blk0 + b, 0, 0)
    new_map = lambda b, c, pt: (blk0 + b, 0)
    in_specs = [pl.BlockSpec((DEC_SEQ, N_HEADS, KV_LORA), q_map),
                pl.BlockSpec((DEC_SEQ, N_HEADS, QK_ROPE), q_map)]
    in_specs += [pl.BlockSpec((None, None, PAGE_SIZE, KV_LORA), page_map(k)) for k in range(S_PAGES)]
    in_specs += [pl.BlockSpec((None, None, PAGE_SIZE, QK_ROPE), page_map(k)) for k in range(S_PAGES)]
    in_specs += [pl.BlockSpec((DEC_SEQ, KV_LORA), new_map),
                 pl.BlockSpec((DEC_SEQ, QK_ROPE), new_map),
                 pl.BlockSpec(memory_space=pl.ANY)]
    grid_spec = pltpu.PrefetchScalarGridSpec(
        num_scalar_prefetch=1,
        grid=(DEC_BATCH, S_CHUNKS),
        in_specs=in_specs,
        out_specs=pl.BlockSpec((DEC_SEQ, N_HEADS, KV_LORA), q_map),
        scratch_shapes=[pltpu.VMEM((S_KEYS, KV_LORA), BF16), pltpu.VMEM((S_KEYS, QK_ROPE), BF16),
                        pltpu.VMEM((S_ROWS, 1), F32), pltpu.VMEM((S_ROWS, 1), F32),
                        pltpu.VMEM((S_ROWS, KV_LORA), F32)],
    )
    n_in = 1 + 2 + 2 * S_PAGES + 3
    return pl.pallas_call(
        _sample_attn_body,
        grid_spec=grid_spec,
        out_shape=jax.ShapeDtypeStruct((T, N_HEADS, KV_LORA), BF16),
        input_output_aliases={n_in - 1: 0},
        compiler_params=_params(2, 40),
        name="sample_attn",
    )(page_table.reshape(-1), q_lat3, q_rope3, *([cache_kv] * S_PAGES), *([cache_kr] * S_PAGES),
      ckv_f32, kr_f32, o_in)


def _ffn(x, h, li, w_gate, w_up, w_down, mod_p, mod_s):
    def ep_glu(accs, ex, outs):
        g, u = accs
        outs[0][...] = (g * jax.nn.sigmoid(g) * u).astype(BF16)

    tn = 256
    act = _mm(h, [w_gate, w_up], ep_glu, tm=1024, tn=tn, n_cols=D_FF,
              out_shapes=[jax.ShapeDtypeStruct((T, D_FF), BF16)],
              out_specs=[_tile_spec(1024, tn)], name="ffn_gate_up", vmem_mib=48, w_layer=li)[0]
    half = D_FF // 2
    part = _partial_mm(act, w_down, li, tm=512, tn=512, tk=half, kblk=0, name="ffn_down_a", vmem_mib=48)
    return _residual_mm(act, w_down, li, x, mod_p, mod_s, 5, tm=512, tn=512, tk=half, kblk=1,
                        partial=part, name="ffn_down_b", vmem_mib=52)


def _ab_layer(x, h, a, state_conv, w_ab_in, w_ab_out, a_spatial_w, a_spatial_b, a_v_norm_g, b_conv_w,
              mod_p, mod_s):
    def ep_in(accs, ex, outs):
        j = pl.program_id(0)
        z = accs[0]

        @pl.when(j < (2 * A_WIDTH) // 512)
        def _():
            outs[0][...] = jax.nn.gelu(z)

        @pl.when(j >= (2 * A_WIDTH) // 512)
        def _():
            outs[0][...] = z

    z = _mm(h, [w_ab_in], ep_in, tm=1024, tn=512, n_cols=AB_IN,
            out_shapes=[jax.ShapeDtypeStruct((T, AB_IN), F32)],
            out_specs=[_tile_spec(1024, 512)], name="ab_in", vmem_mib=48, w_layer=a)[0]

    w_s = a_spatial_w[a]
    b_s = a_spatial_b[a]
    reps = MIX_TM // DEC_SEQ
    ws_s = jnp.tile(w_s[:, :DEC_SEQ, :DEC_SEQ], (1, reps, reps))
    bs_s = jnp.tile(b_s[:, :DEC_SEQ], (1, reps))
    g_v = a_v_norm_g[a].reshape(1, A_WIDTH)
    conv_w = b_conv_w[a]
    e0 = jnp.repeat(state_conv[a, :, 0, :], DEC_SEQ, axis=0)
    e1 = jnp.repeat(state_conv[a, :, 1, :], DEC_SEQ, axis=0)

    yab, tail_p = _mixer(z, True, z, z, w_s, b_s[:, :, None], g_v, conv_w)
    yab, gch_s, v_s = _mixer(z, False, e0, e1, ws_s, bs_s[:, :, None], g_v, conv_w, yab_in=yab)

    x = _residual_mm(yab, w_ab_out, a, x, mod_p, mod_s, 2, tm=512, tn=512, name="ab_out", vmem_mib=44)
    conv_p = tail_p.reshape(BATCH, SEQ // MIX_TM, 8, B_WIDTH)[:, -1, 8 - 2:, :]
    conv_s = gch_s.reshape(DEC_BATCH, DEC_SEQ, B_WIDTH)[:, DEC_SEQ - 2:, :]
    return x, conv_p, conv_s, v_s.reshape(DEC_BATCH, DEC_SEQ, A_WIDTH)


def _rope_tables():
    half = QK_ROPE // 2
    inv = ROPE_THETA ** (-jnp.arange(half, dtype=F32) / half)
    pos_p = jnp.arange(TP, dtype=jnp.int32) % SEQ
    pos_s = PAST_LEN + jnp.arange(TS, dtype=jnp.int32) % DEC_SEQ
    pos = jnp.concatenate([pos_p, pos_s])
    ang = pos.astype(F32)[:, None] * inv[None, :]
    cos, sin = jnp.cos(ang), jnp.sin(ang)
    return jnp.concatenate([cos, cos], axis=-1), jnp.concatenate([-sin, sin], axis=-1)


def _swap_halves(w, n_groups):
    k = w.shape[0]
    w3 = w.reshape(k, n_groups, QK_ROPE)
    half = QK_ROPE // 2
    return jnp.concatenate([w3[..., half:], w3[..., :half]], axis=-1).reshape(k, n_groups * QK_ROPE)


def _mla_layer(x, h, m, cache_kv, cache_kr, page_table, c_w_in, c_q_norm_g, c_kv_norm_g, c_w_q_up,
               c_w_kv_up, c_w_out, mod_p, mod_s, cos64, sin64):
    def ep_cq(accs, ex, outs):
        outs[0][...] = _rms(accs[0], ex[0][...]).astype(BF16)

    cq = _mm(h, [c_w_in], ep_cq, tm=512, tn=Q_LORA, n_cols=Q_LORA,
             out_shapes=[jax.ShapeDtypeStruct((T, Q_LORA), BF16)],
             out_specs=[_tile_spec(512, Q_LORA)], name="mla_in_q", vmem_mib=56, w_layer=m,
             extras=[c_q_norm_g[m].reshape(1, Q_LORA)],
             extra_specs=[pl.BlockSpec((1, Q_LORA), lambda j, i: (0, 0))])[0]

    def ep_ckv(accs, ex, outs):
        y = _rms(accs[0], ex[0][...])
        outs[0][...] = y
        outs[1][...] = y.astype(BF16)

    ckv, ckv_bf = _mm(h, [c_w_in], ep_ckv, tm=1024, tn=KV_LORA, n_cols=KV_LORA,
                      out_shapes=[jax.ShapeDtypeStruct((T, KV_LORA), F32),
                                  jax.ShapeDtypeStruct((T, KV_LORA), BF16)],
                      out_specs=[_tile_spec(1024, KV_LORA)] * 2, name="mla_in_kv", vmem_mib=48,
                      w_layer=m, w_col_off=(Q_LORA // KV_LORA,),
                      extras=[c_kv_norm_g[m].reshape(1, KV_LORA)],
                      extra_specs=[pl.BlockSpec((1, KV_LORA), lambda j, i: (0, 0))])

    def ep_rope(accs, ex, outs):
        y = accs[0] * ex[0][...] + accs[1] * ex[1][...]
        for o in outs:
            o[...] = y.astype(o.dtype)

    w_kr = c_w_in[m, :, Q_LORA + KV_LORA:]
    w_kr_sw = _swap_halves(w_kr, 1)
    kr, kr_bf = _mm(h, [w_kr[None], w_kr_sw[None]], ep_rope, tm=1024, tn=QK_ROPE, n_cols=QK_ROPE,
                    out_shapes=[jax.ShapeDtypeStruct((T, QK_ROPE), F32),
                                jax.ShapeDtypeStruct((T, QK_ROPE), BF16)],
                    out_specs=[_tile_spec(1024, QK_ROPE)] * 2, name="mla_in_kr", vmem_mib=40,
                    extras=[cos64, sin64],
                    extra_specs=[pl.BlockSpec((1024, QK_ROPE), lambda j, i: (i, 0))] * 2)

    w_q3 = c_w_q_up[m].reshape(Q_LORA, N_HEADS, QK_NOPE + QK_ROPE)
    w_q_nope = w_q3[:, :, :QK_NOPE].reshape(Q_LORA, N_HEADS * QK_NOPE)
    w_q_rope = w_q3[:, :, QK_NOPE:].reshape(Q_LORA, N_HEADS * QK_ROPE)
    w_q_rope_sw = _swap_halves(w_q_rope, N_HEADS)

    def ep_cast(accs, ex, outs):
        outs[0][...] = accs[0].astype(BF16)

    q_nope = _mm(cq, [w_q_nope[None]], ep_cast, tm=1024, tn=512, n_cols=N_HEADS * QK_NOPE,
                 out_shapes=[jax.ShapeDtypeStruct((T, N_HEADS * QK_NOPE), BF16)],
                 out_specs=[_tile_spec(1024, 512)], name="mla_q_nope", vmem_mib=32)[0]
    reps = 512 // QK_ROPE
    cos512 = jnp.tile(cos64, (1, reps))
    sin512 = jnp.tile(sin64, (1, reps))
    q_rope = _mm(cq, [w_q_rope[None], w_q_rope_sw[None]], ep_rope, tm=1024, tn=512,
                 n_cols=N_HEADS * QK_ROPE,
                 out_shapes=[jax.ShapeDtypeStruct((T, N_HEADS * QK_ROPE), BF16)],
                 out_specs=[_tile_spec(1024, 512)], name="mla_q_rope", vmem_mib=32,
                 extras=[cos512, sin512],
                 extra_specs=[pl.BlockSpec((1024, 512), lambda j, i: (i, 0))] * 2)[0]

    w_kv3 = c_w_kv_up[m].reshape(KV_LORA, N_HEADS, QK_NOPE + V_HEAD)
    w_uk_t = jnp.transpose(w_kv3[:, :, :QK_NOPE], (1, 2, 0))
    w_uv = jnp.transpose(w_kv3[:, :, QK_NOPE:], (1, 0, 2))
    q_lat = _head_mm(q_nope, w_uk_t, QK_NOPE, KV_LORA, "mla_absorb")

    q_lat3 = q_lat.reshape(T, N_HEADS, KV_LORA)
    q_rope3 = q_rope.reshape(T, N_HEADS, QK_ROPE)
    o_lat = _prompt_attn(q_lat3, q_rope3, ckv_bf, kr_bf)
    o_lat = _sample_attn(page_table, q_lat3, q_rope3, cache_kv, cache_kr, m, ckv, kr, o_lat)

    o = _head_mm(o_lat.reshape(T, N_HEADS * KV_LORA), w_uv, KV_LORA, V_HEAD, "mla_uv")
    x = _residual_mm(o, c_w_out, m, x, mod_p, mod_s, 2, tm=512, tn=512, name="mla_out", vmem_mib=44)
    return x, ckv, kr


def kernel(x_prompt, x_sample, state_conv, cache_kv_latent, cache_k_rope, page_table, c_prompt, c_sample,
           norm_g, w_mod, b_mod, w_ab_in, w_ab_out, a_spatial_w, a_spatial_b, a_v_norm_g, b_conv_w,
           c_w_in, c_q_norm_g, c_kv_norm_g, c_w_q_up, c_w_kv_up, c_w_out,
           ffn_w_gate, ffn_w_up, ffn_w_down, final_norm_g):
    depth = norm_g.shape[0]
    x = jnp.concatenate([x_prompt.reshape(TP, D_MODEL), x_sample.reshape(TS, D_MODEL)], axis=0)
    c_all = jnp.concatenate([c_prompt, c_sample, jnp.zeros((N_SEQ_PAD - N_SEQ, D_MODEL), F32)], axis=0)
    mod = _adaln(c_all, w_mod, b_mod)
    g_all = norm_g.reshape(depth * 2, 1, D_MODEL)
    cos64, sin64 = _rope_tables()

    conv_p, conv_s, chunk_v_s = [], [], []
    kv_p, kr_p, kv_s, kr_s = [], [], [], []
    for li in range(depth):
        mod_p = mod[li, :BATCH].reshape(BATCH, 1, N_MOD * D_MODEL)
        mod_s = jnp.repeat(mod[li, BATCH:N_SEQ], DEC_SEQ, axis=0)
        h = _norm_mod(x, g_all, li * 2, mod_p, mod_s, 0)
        if li % 2 == 0:
            a = li // 2
            x, cp, cs, vs = _ab_layer(x, h, a, state_conv, w_ab_in, w_ab_out, a_spatial_w, a_spatial_b,
                                      a_v_norm_g, b_conv_w, mod_p, mod_s)
            conv_p.append(cp)
            conv_s.append(cs)
            chunk_v_s.append(vs)
        else:
            m = li // 2
            x, ckv, kr = _mla_layer(x, h, m, cache_kv_latent, cache_k_rope, page_table, c_w_in, c_q_norm_g,
                                    c_kv_norm_g, c_w_q_up, c_w_kv_up, c_w_out, mod_p, mod_s, cos64, sin64)
            kv_p.append(ckv[:TP].reshape(BATCH, SEQ, KV_LORA))
            kr_p.append(kr[:TP].reshape(BATCH, SEQ, QK_ROPE))
            kv_s.append(ckv[TP:].reshape(DEC_BATCH, DEC_SEQ, KV_LORA))
            kr_s.append(kr[TP:].reshape(DEC_BATCH, DEC_SEQ, QK_ROPE))
        h = _norm_mod(x, g_all, li * 2 + 1, mod_p, mod_s, 3)
        x = _ffn(x, h, li, ffn_w_gate, ffn_w_up, ffn_w_down, mod_p, mod_s)

    y_prompt = _final_norm(x, final_norm_g, 0, TP, (BATCH, SEQ, D_MODEL))
    y_sample = _final_norm(x, final_norm_g, TP // NORM_TM, TS, (DEC_BATCH, DEC_SEQ, D_MODEL))
    return (y_prompt, y_sample, jnp.stack(conv_p), jnp.stack(conv_s), jnp.stack(chunk_v_s),
            jnp.stack(kv_p), jnp.stack(kr_p), jnp.stack(kv_s), jnp.stack(kr_s))
```

```python
import functools
import math

import jax
import jax.numpy as jnp
from jax import lax
from jax.experimental import pallas as pl
from jax.experimental.pallas import tpu as pltpu

F32 = jnp.float32
BF16 = jnp.bfloat16

D_MODEL = 4096
BATCH = 4
SEQ = 2048
DEC_BATCH = 128
DEC_SEQ = 8
PAST_LEN = 16384
PAGE_SIZE = 128
N_PAGES = PAST_LEN // PAGE_SIZE
CHUNK = 128
A_GROUPS = 8
A_WIDTH = D_MODEL // 2
A_GROUP_DIM = A_WIDTH // A_GROUPS
B_WIDTH = D_MODEL // 2
AB_IN = 2 * A_WIDTH + 3 * B_WIDTH
N_HEADS = 32
QK_NOPE = 128
QK_ROPE = 64
V_HEAD = 128
Q_LORA = 1024
KV_LORA = 512
ROPE_THETA = 10000.0
ATTN_SCALE = 1.0 / math.sqrt(QK_NOPE + QK_ROPE)
D_FF = 11008
N_MOD = 6
EPS = 1e-6

TP = BATCH * SEQ
TS = DEC_BATCH * DEC_SEQ
T = TP + TS
N_SEQ = BATCH + DEC_BATCH
N_SEQ_PAD = 136

MIB = 1024 * 1024
W_CONVERT_ROWS = 512

ARB = "arbitrary"


def _params(n_axes, vmem_mib):
    return pltpu.CompilerParams(dimension_semantics=(ARB,) * n_axes,
                                vmem_limit_bytes=vmem_mib * MIB)


def _dot(a, b):
    return jnp.dot(a, b, preferred_element_type=F32)


def _dot_nt(a, b):
    return lax.dot_general(a, b, (((1,), (1,)), ((), ())), preferred_element_type=F32)


ADALN_TN = 512


def _adaln_body(c_ref, w_ref, b_ref, o_ref):
    c = c_ref[...]
    a = (c * jax.nn.sigmoid(c)).astype(BF16)
    w = w_ref[...].astype(BF16)
    o_ref[...] = _dot(a, w) + b_ref[...]


def _adaln(c_all, w_mod, b_mod):
    depth = w_mod.shape[0]
    n = w_mod.shape[2]
    return pl.pallas_call(
        _adaln_body,
        grid=(depth, n // ADALN_TN),
        in_specs=[
            pl.BlockSpec((N_SEQ_PAD, D_MODEL), lambda l, j: (0, 0)),
            pl.BlockSpec((None, D_MODEL, ADALN_TN), lambda l, j: (l, 0, j)),
            pl.BlockSpec((None, 1, ADALN_TN), lambda l, j: (l, 0, j)),
        ],
        out_specs=pl.BlockSpec((None, N_SEQ_PAD, ADALN_TN), lambda l, j: (l, 0, j)),
        out_shape=jax.ShapeDtypeStruct((depth, N_SEQ_PAD, n), F32),
        compiler_params=_params(2, 40),
        name="adaln",
    )(c_all, w_mod, b_mod.reshape(depth, 1, n))


def _mod_specs(tm, tn, col_blk, ij):
    n_p_tiles = TP // tm
    tiles_per_seq = SEQ // tm

    def p_map(*g):
        i, j = ij(*g)
        return (jnp.minimum(i // tiles_per_seq, BATCH - 1), 0, col_blk(j))

    def s_map(*g):
        i, j = ij(*g)
        return (jnp.maximum(i - n_p_tiles, 0), col_blk(j))

    return [pl.BlockSpec((None, 1, tn), p_map), pl.BlockSpec((tm, tn), s_map)]


def _pick_mod(i, tm, p_ref, s_ref):
    return jnp.where(i < TP // tm, p_ref[...], s_ref[...])


NORM_TM = 256


def _rms(x, g):
    return x * lax.rsqrt(jnp.mean(x * x, axis=-1, keepdims=True) + EPS) * g


def _norm_mod_body(x_ref, g_ref, shp_ref, shs_ref, scp_ref, scs_ref, o_ref):
    i = pl.program_id(0)
    y = _rms(x_ref[...], g_ref[...])
    shift = _pick_mod(i, NORM_TM, shp_ref, shs_ref)
    scale = _pick_mod(i, NORM_TM, scp_ref, scs_ref)
    o_ref[...] = (y * (1.0 + scale) + shift).astype(BF16)


def _norm_mod(x, g_all, g_idx, mod_p, mod_s, shift_chunk):
    ij = lambda i: (i, 0)
    specs = (_mod_specs(NORM_TM, D_MODEL, lambda j: shift_chunk, ij)
             + _mod_specs(NORM_TM, D_MODEL, lambda j: shift_chunk + 1, ij))
    return pl.pallas_call(
        _norm_mod_body,
        grid=(T // NORM_TM,),
        in_specs=[pl.BlockSpec((NORM_TM, D_MODEL), lambda i: (i, 0)),
                  pl.BlockSpec((None, 1, D_MODEL), lambda i: (g_idx, 0, 0))] + specs,
        out_specs=pl.BlockSpec((NORM_TM, D_MODEL), lambda i: (i, 0)),
        out_shape=jax.ShapeDtypeStruct((T, D_MODEL), BF16),
        compiler_params=_params(1, 48),
        name="norm_mod",
    )(x, g_all, mod_p, mod_s, mod_p, mod_s)


def _final_norm_body(x_ref, g_ref, o_ref):
    o_ref[...] = _rms(x_ref[...], g_ref[...])


def _final_norm(x, g, row_blk0, n_rows, out_shape3):
    s = out_shape3[1]
    if s >= NORM_TM:
        per = s // NORM_TM
        o_spec = pl.BlockSpec((None, NORM_TM, D_MODEL), lambda i: (i // per, i % per, 0))
        out = jax.ShapeDtypeStruct(out_shape3, F32)
    else:
        o_spec = pl.BlockSpec((NORM_TM, D_MODEL), lambda i: (i, 0))
        out = jax.ShapeDtypeStruct((n_rows, D_MODEL), F32)
    y = pl.pallas_call(
        _final_norm_body,
        grid=(n_rows // NORM_TM,),
        in_specs=[pl.BlockSpec((NORM_TM, D_MODEL), lambda i: (i + row_blk0, 0)),
                  pl.BlockSpec((1, D_MODEL), lambda i: (0, 0))],
        out_specs=o_spec,
        out_shape=out,
        compiler_params=_params(1, 32),
        name="final_norm",
    )(x, g.reshape(1, D_MODEL))
    return y.reshape(out_shape3)


def _mm(a, ws, epilogue, *, tm, tn, n_cols, out_shapes, out_specs, name, vmem_mib,
        w_layer=0, w_col_off=None, tk=None, kblk=0, extras=(), extra_specs=()):
    m = a.shape[0]
    tk = a.shape[1] if tk is None else tk
    nw, ne, no = len(ws), len(extras), len(out_shapes)
    w_col_off = (0,) * nw if w_col_off is None else w_col_off

    def body(*refs):
        a_ref = refs[0]
        w_refs = refs[1:1 + nw]
        ex = refs[1 + nw:1 + nw + ne]
        outs = refs[1 + nw + ne:1 + nw + ne + no]
        wbf = refs[1 + nw + ne + no:]

        @pl.when(pl.program_id(1) == 0)
        def _():
            for w, wb in zip(w_refs, wbf):
                for r in range(0, tk, W_CONVERT_ROWS):
                    rows = min(W_CONVERT_ROWS, tk - r)
                    wb[r:r + rows, :] = w[r:r + rows, :].astype(BF16)

        av = a_ref[...]
        accs = [_dot(av, wb[...]) for wb in wbf]
        epilogue(accs, ex, outs)

    in_specs = [pl.BlockSpec((tm, tk), lambda j, i: (i, kblk))]
    for off in w_col_off:
        in_specs.append(pl.BlockSpec((None, tk, tn), lambda j, i, off=off: (w_layer, kblk, j + off)))
    in_specs += list(extra_specs)
    return pl.pallas_call(
        body,
        grid=(n_cols // tn, m // tm),
        in_specs=in_specs,
        out_specs=out_specs,
        out_shape=out_shapes,
        scratch_shapes=[pltpu.VMEM((tk, tn), BF16) for _ in range(nw)],
        compiler_params=_params(2, vmem_mib),
        name=name,
    )(a, *ws, *extras)


def _tile_spec(tm, tn, col_off=0):
    return pl.BlockSpec((tm, tn), lambda j, i: (i, j + col_off))


def _residual_mm(a, w, w_layer, x, mod_p, mod_s, gate_chunk, *, tm, tn, name, vmem_mib,
                 tk=None, kblk=0, partial=None):
    ij = lambda j, i: (i, j)
    gate_specs = _mod_specs(tm, tn, lambda j: gate_chunk * (D_MODEL // tn) + j, ij)
    has_partial = partial is not None

    def epilogue(accs, ex, outs):
        i = pl.program_id(1)
        acc = accs[0]
        if has_partial:
            acc = acc + ex[3][...]
        gate = _pick_mod(i, tm, ex[1], ex[2])
        outs[0][...] = ex[0][...] + gate * acc

    extras = [x, mod_p, mod_s] + ([partial] if has_partial else [])
    extra_specs = [_tile_spec(tm, tn)] + gate_specs + ([_tile_spec(tm, tn)] if has_partial else [])
    return _mm(a, [w], epilogue, tm=tm, tn=tn, n_cols=D_MODEL,
               out_shapes=[jax.ShapeDtypeStruct((T, D_MODEL), F32)],
               out_specs=[_tile_spec(tm, tn)], name=name, vmem_mib=vmem_mib,
               w_layer=w_layer, tk=tk, kblk=kblk, extras=extras, extra_specs=extra_specs)[0]


def _partial_mm(a, w, w_layer, *, tm, tn, tk, kblk, name, vmem_mib):
    def epilogue(accs, ex, outs):
        outs[0][...] = accs[0]

    return _mm(a, [w], epilogue, tm=tm, tn=tn, n_cols=D_MODEL,
               out_shapes=[jax.ShapeDtypeStruct((T, D_MODEL), F32)],
               out_specs=[_tile_spec(tm, tn)], name=name, vmem_mib=vmem_mib,
               w_layer=w_layer, tk=tk, kblk=kblk)[0]


MIX_TM = CHUNK


def _layer_norm(x, g):
    mu = jnp.mean(x, axis=-1, keepdims=True)
    xc = x - mu
    return xc * lax.rsqrt(jnp.mean(xc * xc, axis=-1, keepdims=True) + EPS) * g


def _mixer_body(is_prompt, u_ref, vg_ref, gb_ref, gc_ref, hi_ref, h1_ref, h2_ref,
                ws_ref, bs_ref, gv_ref, cw_ref, *rest):
    if is_prompt:
        yab_ref, tail_ref = rest
    else:
        _, yab_ref, tail_ref, v_ref = rest
    i = pl.program_id(0)
    v = _layer_norm(vg_ref[...], gv_ref[...])
    if not is_prompt:
        v_ref[...] = v
    vb = v.astype(BF16)
    row = lax.broadcasted_iota(jnp.int32, (MIX_TM, MIX_TM), 0)
    col = lax.broadcasted_iota(jnp.int32, (MIX_TM, MIX_TM), 1)
    mask = col <= row
    if not is_prompt:
        mask = mask & ((col // DEC_SEQ) == (row // DEC_SEQ))
    for g in range(A_GROUPS):
        sl = slice(g * A_GROUP_DIM, (g + 1) * A_GROUP_DIM)
        w = jnp.where(mask, ws_ref[g], 0.0).astype(BF16)
        s = _dot(w, vb[:, sl]) + bs_ref[g]
        yab_ref[:, sl] = (u_ref[:, sl] * s).astype(BF16)

    gch = gc_ref[...] * hi_ref[...]
    pos = lax.broadcasted_iota(jnp.int32, (MIX_TM, B_WIDTH), 0)
    r1 = pltpu.roll(gch, 1, 0)
    r2 = pltpu.roll(gch, 2, 0)
    if is_prompt:
        halo = jnp.where(i % (SEQ // MIX_TM) == 0, 0.0, h1_ref[...] * h2_ref[...])
        p1 = halo[7:8, :]
        p2 = halo[6:7, :]
        x1 = jnp.where(pos == 0, p1, r1)
        x2 = jnp.where(pos == 0, p2, jnp.where(pos == 1, p1, r2))
    else:
        pos = pos % DEC_SEQ
        e0 = h1_ref[...]
        e1 = h2_ref[...]
        x1 = jnp.where(pos == 0, e1, r1)
        x2 = jnp.where(pos == 0, e0, jnp.where(pos == 1, e1, r2))
    yc = x2 * cw_ref[0:1, :] + x1 * cw_ref[1:2, :] + gch * cw_ref[2:3, :]
    yab_ref[:, A_WIDTH:] = (gb_ref[...] * yc).astype(BF16)
    if is_prompt:
        tail_ref[...] = gch[MIX_TM - 8:, :]
    else:
        tail_ref[...] = gch


def _mixer(z, is_prompt, halo_a, halo_b, ws_eff, bs_eff, g_v, conv_w, yab_in=None):
    n_tiles = (TP if is_prompt else TS) // MIX_TM
    blk0 = 0 if is_prompt else TP // MIX_TM
    zc = lambda c: pl.BlockSpec((MIX_TM, A_WIDTH), lambda i, c=c: (i + blk0, c))
    if is_prompt:
        rows8 = MIX_TM // 8
        halo_specs = [pl.BlockSpec((8, B_WIDTH), lambda i: (jnp.maximum(i * rows8 - 1, 0), 3)),
                      pl.BlockSpec((8, B_WIDTH), lambda i: (jnp.maximum(i * rows8 - 1, 0), 4))]
    else:
        halo_specs = [pl.BlockSpec((MIX_TM, B_WIDTH), lambda i: (i, 0)),
                      pl.BlockSpec((MIX_TM, B_WIDTH), lambda i: (i, 0))]
    in_specs = [zc(0), zc(1), zc(2), zc(3), zc(4)] + halo_specs + [
        pl.BlockSpec((A_GROUPS, MIX_TM, MIX_TM), lambda i: (0, 0, 0)),
        pl.BlockSpec((A_GROUPS, MIX_TM, 1), lambda i: (0, 0, 0)),
        pl.BlockSpec((1, A_WIDTH), lambda i: (0, 0)),
        pl.BlockSpec((3, B_WIDTH), lambda i: (0, 0)),
    ]
    args = [z, z, z, z, z, halo_a, halo_b, ws_eff, bs_eff, g_v, conv_w]
    yab_spec = pl.BlockSpec((MIX_TM, D_MODEL), lambda i: (i + blk0, 0))
    yab_shape = jax.ShapeDtypeStruct((T, D_MODEL), BF16)
    if is_prompt:
        out_shapes = [yab_shape, jax.ShapeDtypeStruct((n_tiles * 8, B_WIDTH), F32)]
        out_specs = [yab_spec, pl.BlockSpec((8, B_WIDTH), lambda i: (i, 0))]
        aliases = {}
    else:
        in_specs.append(pl.BlockSpec(memory_space=pl.ANY))
        args.append(yab_in)
        out_shapes = [yab_shape, jax.ShapeDtypeStruct((TS, B_WIDTH), F32),
                      jax.ShapeDtypeStruct((TS, A_WIDTH), F32)]
        out_specs = [yab_spec, pl.BlockSpec((MIX_TM, B_WIDTH), lambda i: (i, 0)),
                     pl.BlockSpec((MIX_TM, A_WIDTH), lambda i: (i, 0))]
        aliases = {len(args) - 1: 0}
    return pl.pallas_call(
        functools.partial(_mixer_body, is_prompt),
        grid=(n_tiles,),
        in_specs=in_specs,
        out_specs=out_specs,
        out_shape=out_shapes,
        input_output_aliases=aliases,
        compiler_params=_params(1, 40),
        name="mixer_prompt" if is_prompt else "mixer_sample",
    )(*args)


Q_TM = 1024
Q_HEADS = 4
Q_SAMPLE_TILE = TP // Q_TM


def _q_lat_body(cq_ref, wq_ref, wuk_ref, o_ref, os_ref, wq_bf, wuk_bf):
    i = pl.program_id(1)

    @pl.when(i == 0)
    def _():
        wq_bf[...] = wq_ref[...].astype(BF16)
        wuk_bf[...] = wuk_ref[...].astype(BF16)

    qn = _dot(cq_ref[...], wq_bf[...]).astype(BF16)
    for hh in range(Q_HEADS):
        ql = _dot(qn[:, hh * QK_NOPE:(hh + 1) * QK_NOPE], wuk_bf[hh])
        o_ref[:, hh * KV_LORA:(hh + 1) * KV_LORA] = ql.astype(BF16)

        @pl.when(i == Q_SAMPLE_TILE)
        def _():
            os_ref[:, hh * KV_LORA:(hh + 1) * KV_LORA] = ql


def _q_lat(cq, w_q_nope, w_uk_t):
    tn_in = Q_HEADS * QK_NOPE
    tn_out = Q_HEADS * KV_LORA
    return pl.pallas_call(
        _q_lat_body,
        grid=(N_HEADS // Q_HEADS, T // Q_TM),
        in_specs=[pl.BlockSpec((Q_TM, Q_LORA), lambda j, i: (i, 0)),
                  pl.BlockSpec((Q_LORA, tn_in), lambda j, i: (0, j)),
                  pl.BlockSpec((Q_HEADS, QK_NOPE, KV_LORA), lambda j, i: (j, 0, 0))],
        out_specs=[pl.BlockSpec((Q_TM, tn_out), lambda j, i: (i, j)),
                   pl.BlockSpec((TS, tn_out), lambda j, i: (0, j))],
        out_shape=[jax.ShapeDtypeStruct((T, N_HEADS * KV_LORA), BF16),
                   jax.ShapeDtypeStruct((TS, N_HEADS * KV_LORA), F32)],
        scratch_shapes=[pltpu.VMEM((Q_LORA, tn_in), BF16),
                        pltpu.VMEM((Q_HEADS, QK_NOPE, KV_LORA), BF16)],
        compiler_params=_params(2, 48),
        name="mla_q_lat",
    )(cq, w_q_nope, w_uk_t)


def _uv_sample_body(a_ref, w_ref, _, o_ref):
    o_ref[...] = _dot(a_ref[...].astype(BF16), w_ref[...]).astype(BF16)


def _uv_sample(o_lat_s, w_uv_bf, o_in):
    return pl.pallas_call(
        _uv_sample_body,
        grid=(N_HEADS,),
        in_specs=[pl.BlockSpec((TS, KV_LORA), lambda h: (0, h)),
                  pl.BlockSpec((None, KV_LORA, V_HEAD), lambda h: (h, 0, 0)),
                  pl.BlockSpec(memory_space=pl.ANY)],
        out_specs=pl.BlockSpec((TS, V_HEAD), lambda h: (TP // TS, h)),
        out_shape=jax.ShapeDtypeStruct((T, N_HEADS * V_HEAD), BF16),
        input_output_aliases={2: 0},
        compiler_params=_params(1, 32),
        name="mla_uv_sample",
    )(o_lat_s, w_uv_bf, o_in)


PQ = 128
PK = 256
P_ROWS = PQ * N_HEADS
PR = 512
PR_HEADS = PR // PQ


def _prompt_attend(nk, qb, qs, qrs, kv_ref, kr_ref, wuv_ref, o3):
    def scores(c):
        r0 = pl.multiple_of(c * PR, PR)
        return _dot_nt(qs[pl.ds(r0, PR), :], kv_ref[0:nk, :]) + _dot_nt(qrs[pl.ds(r0, PR), :], kr_ref[0:nk, :])

    def finish(c, s):
        kv = kv_ref[0:nk, :]
        tok = qb * PQ + lax.broadcasted_iota(jnp.int32, (PR, PK), 0) % PQ
        key = (nk - PK) + lax.broadcasted_iota(jnp.int32, (PR, PK), 1)
        tail = jnp.where(key <= tok, s[:, nk - PK:], -jnp.inf)
        m = jnp.max(tail, axis=-1, keepdims=True)
        if nk > PK:
            head = s[:, :nk - PK]
            m = jnp.maximum(m, jnp.max(head, axis=-1, keepdims=True))
            p_head = jnp.exp((head - m) * ATTN_SCALE)
            p_tail = jnp.exp((tail - m) * ATTN_SCALE)
            l = jnp.sum(p_head, axis=-1, keepdims=True) + jnp.sum(p_tail, axis=-1, keepdims=True)
            p = jnp.concatenate([p_head.astype(BF16), p_tail.astype(BF16)], axis=1)
        else:
            p_tail = jnp.exp((tail - m) * ATTN_SCALE)
            l = jnp.sum(p_tail, axis=-1, keepdims=True)
            p = p_tail.astype(BF16)
        o_lat = (_dot(p, kv) * (1.0 / l)).astype(BF16)
        for hh in range(PR_HEADS):
            h = c * PR_HEADS + hh
            o3[h] = _dot(o_lat[hh * PQ:(hh + 1) * PQ, :], wuv_ref[h]).astype(BF16)

    def pair(cp, carry):
        s_a = scores(2 * cp)
        s_b = scores(2 * cp + 1)
        finish(2 * cp, s_a)
        finish(2 * cp + 1, s_b)
        return carry

    lax.fori_loop(0, P_ROWS // PR // 2, pair, 0)


def _prompt_attn_body(ql_ref, qr_ref, kv_ref, kr_ref, wuv_ref, o_ref, qs, qrs, o3):
    qb = pl.program_id(1)
    for h in range(N_HEADS):
        qs[h * PQ:(h + 1) * PQ, :] = ql_ref[:, h * KV_LORA:(h + 1) * KV_LORA]
        qrs[h * PQ:(h + 1) * PQ, :] = qr_ref[:, h * QK_ROPE:(h + 1) * QK_ROPE]

    for v in range(SEQ // PK):
        @pl.when((qb * PQ) // PK == v)
        def _():
            _prompt_attend((v + 1) * PK, qb, qs, qrs, kv_ref, kr_ref, wuv_ref, o3)

    for h in range(N_HEADS):
        o_ref[:, h * V_HEAD:(h + 1) * V_HEAD] = o3[h]


def _prompt_attn(q_lat, q_rope, ckv_bf, kr_bf, w_uv_bf):
    nq = SEQ // PQ
    q_map = lambda b, qb: (b * nq + qb, 0)
    return pl.pallas_call(
        _prompt_attn_body,
        grid=(BATCH, nq),
        in_specs=[pl.BlockSpec((PQ, N_HEADS * KV_LORA), q_map),
                  pl.BlockSpec((PQ, N_HEADS * QK_ROPE), q_map),
                  pl.BlockSpec((SEQ, KV_LORA), lambda b, qb: (b, 0)),
                  pl.BlockSpec((SEQ, QK_ROPE), lambda b, qb: (b, 0)),
                  pl.BlockSpec((N_HEADS, KV_LORA, V_HEAD), lambda b, qb: (0, 0, 0))],
        out_specs=pl.BlockSpec((PQ, N_HEADS * V_HEAD), q_map),
        out_shape=jax.ShapeDtypeStruct((T, N_HEADS * V_HEAD), BF16),
        scratch_shapes=[pltpu.VMEM((P_ROWS, KV_LORA), BF16), pltpu.VMEM((P_ROWS, QK_ROPE), BF16),
                        pltpu.VMEM((N_HEADS, PQ, V_HEAD), BF16)],
        compiler_params=_params(2, 56),
        name="prompt_attn",
    )(q_lat, q_rope, ckv_bf, kr_bf, w_uv_bf)


S_PAGES = 16
S_CHUNKS = N_PAGES // S_PAGES
S_ROWS = DEC_SEQ * N_HEADS


S_SUB = 4
S_SUBS = S_PAGES // S_SUB


def _online_softmax(s, v, m, l, acc):
    m_new = jnp.maximum(m, jnp.max(s, axis=-1, keepdims=True))
    alpha = jnp.exp((m - m_new) * ATTN_SCALE)
    p = jnp.exp((s - m_new) * ATTN_SCALE)
    l = alpha * l + jnp.sum(p, axis=-1, keepdims=True)
    acc = alpha * acc + _dot(p.astype(BF16), v)
    return m_new, l, acc


def _sample_attn_body(pt_ref, ql_ref, qr_ref, *rest):
    kv_pages = rest[:S_PAGES]
    kr_pages = rest[S_PAGES:2 * S_PAGES]
    kvn_ref, krn_ref, o_ref, qs, qrs, m_ref, l_ref, acc_ref = rest[2 * S_PAGES:]
    c = pl.program_id(1)

    @pl.when(c == 0)
    def _():
        for h in range(N_HEADS):
            qs[h * DEC_SEQ:(h + 1) * DEC_SEQ, :] = ql_ref[:, h * KV_LORA:(h + 1) * KV_LORA]
            qrs[h * DEC_SEQ:(h + 1) * DEC_SEQ, :] = qr_ref[:, h * QK_ROPE:(h + 1) * QK_ROPE]
        m_ref[...] = jnp.full_like(m_ref, -jnp.inf)
        l_ref[...] = jnp.zeros_like(l_ref)
        acc_ref[...] = jnp.zeros_like(acc_ref)

    q = qs[...].astype(BF16)
    qr = qrs[...].astype(BF16)
    m, l, acc = m_ref[...], l_ref[...], acc_ref[...]
    def scores(sb):
        pages = range(sb * S_SUB, (sb + 1) * S_SUB)
        kv = jnp.concatenate([kv_pages[k][...].astype(BF16) for k in pages], axis=0)
        kr_t = jnp.concatenate([kr_pages[k][...].astype(BF16) for k in pages], axis=1)
        return _dot_nt(q, kv) + _dot(qr, kr_t), kv

    pending = scores(0)
    for sb in range(S_SUBS):
        following = scores(sb + 1) if sb + 1 < S_SUBS else None
        m, l, acc = _online_softmax(*pending, m, l, acc)
        pending = following
    m_ref[...] = m
    l_ref[...] = l
    acc_ref[...] = acc

    @pl.when(c == S_CHUNKS - 1)
    def _():
        pad = PAGE_SIZE - DEC_SEQ
        kvn = jnp.concatenate([kvn_ref[...], jnp.zeros((pad, KV_LORA), F32)], axis=0).astype(BF16)
        krn = jnp.concatenate([krn_ref[...], jnp.zeros((pad, QK_ROPE), F32)], axis=0).astype(BF16)
        sn = _dot_nt(q, kvn) + _dot_nt(qr, krn)
        tok = lax.broadcasted_iota(jnp.int32, (S_ROWS, PAGE_SIZE), 0) % DEC_SEQ
        key = lax.broadcasted_iota(jnp.int32, (S_ROWS, PAGE_SIZE), 1)
        sn = jnp.where(key <= tok, sn, -jnp.inf)
        _, l2, acc2 = _online_softmax(sn, kvn, m, l, acc)
        o = acc2 * (1.0 / l2)
        for h in range(N_HEADS):
            o_ref[:, h * KV_LORA:(h + 1) * KV_LORA] = o[h * DEC_SEQ:(h + 1) * DEC_SEQ, :]


def _sample_attn(page_table, q_lat_s, q_rope_s, cache_kv, cache_kr_t, layer, ckv_f32, kr_f32):
    def page_map(k):
        return lambda b, c, pt: (layer, pt[b * N_PAGES + c * S_PAGES + k], 0, 0)

    q_map = lambda b, c, pt: (b, 0)
    new_map = lambda b, c, pt: (TP // DEC_SEQ + b, 0)
    in_specs = [pl.BlockSpec((DEC_SEQ, N_HEADS * KV_LORA), q_map),
                pl.BlockSpec((DEC_SEQ, N_HEADS * QK_ROPE), q_map)]
    in_specs += [pl.BlockSpec((None, None, PAGE_SIZE, KV_LORA), page_map(k)) for k in range(S_PAGES)]
    in_specs += [pl.BlockSpec((None, None, QK_ROPE, PAGE_SIZE), page_map(k)) for k in range(S_PAGES)]
    in_specs += [pl.BlockSpec((DEC_SEQ, KV_LORA), new_map),
                 pl.BlockSpec((DEC_SEQ, QK_ROPE), new_map)]
    grid_spec = pltpu.PrefetchScalarGridSpec(
        num_scalar_prefetch=1,
        grid=(DEC_BATCH, S_CHUNKS),
        in_specs=in_specs,
        out_specs=pl.BlockSpec((DEC_SEQ, N_HEADS * KV_LORA), q_map),
        scratch_shapes=[pltpu.VMEM((S_ROWS, KV_LORA), F32), pltpu.VMEM((S_ROWS, QK_ROPE), F32),
                        pltpu.VMEM((S_ROWS, 1), F32), pltpu.VMEM((S_ROWS, 1), F32),
                        pltpu.VMEM((S_ROWS, KV_LORA), F32)],
    )
    return pl.pallas_call(
        _sample_attn_body,
        grid_spec=grid_spec,
        out_shape=jax.ShapeDtypeStruct((TS, N_HEADS * KV_LORA), F32),
        compiler_params=_params(2, 40),
        name="sample_attn",
    )(page_table.reshape(-1), q_lat_s, q_rope_s, *([cache_kv] * S_PAGES), *([cache_kr_t] * S_PAGES),
      ckv_f32, kr_f32)


def _ffn(x, h, li, w_gate, w_up, w_down, mod_p, mod_s):
    def ep_glu(accs, ex, outs):
        g, u = accs
        outs[0][...] = (g * jax.nn.sigmoid(g) * u).astype(BF16)

    tn = 256
    act = _mm(h, [w_gate, w_up], ep_glu, tm=1024, tn=tn, n_cols=D_FF,
              out_shapes=[jax.ShapeDtypeStruct((T, D_FF), BF16)],
              out_specs=[_tile_spec(1024, tn)], name="ffn_gate_up", vmem_mib=48, w_layer=li)[0]
    half = D_FF // 2
    part = _partial_mm(act, w_down, li, tm=512, tn=512, tk=half, kblk=0, name="ffn_down_a", vmem_mib=48)
    return _residual_mm(act, w_down, li, x, mod_p, mod_s, 5, tm=512, tn=512, tk=half, kblk=1,
                        partial=part, name="ffn_down_b", vmem_mib=52)


def _ab_layer(x, h, a, state_conv, w_ab_in, w_ab_out, a_spatial_w, a_spatial_b, a_v_norm_g, b_conv_w,
              mod_p, mod_s):
    def ep_in(accs, ex, outs):
        j = pl.program_id(0)
        z = accs[0]

        @pl.when(j < (2 * A_WIDTH) // 512)
        def _():
            outs[0][...] = jax.nn.gelu(z)

        @pl.when(j >= (2 * A_WIDTH) // 512)
        def _():
            outs[0][...] = z

    z = _mm(h, [w_ab_in], ep_in, tm=1024, tn=512, n_cols=AB_IN,
            out_shapes=[jax.ShapeDtypeStruct((T, AB_IN), F32)],
            out_specs=[_tile_spec(1024, 512)], name="ab_in", vmem_mib=48, w_layer=a)[0]

    w_s = a_spatial_w[a]
    b_s = a_spatial_b[a]
    reps = MIX_TM // DEC_SEQ
    ws_s = jnp.tile(w_s[:, :DEC_SEQ, :DEC_SEQ], (1, reps, reps))
    bs_s = jnp.tile(b_s[:, :DEC_SEQ], (1, reps))
    g_v = a_v_norm_g[a].reshape(1, A_WIDTH)
    conv_w = b_conv_w[a]
    e0 = jnp.repeat(state_conv[a, :, 0, :], DEC_SEQ, axis=0)
    e1 = jnp.repeat(state_conv[a, :, 1, :], DEC_SEQ, axis=0)

    yab, tail_p = _mixer(z, True, z, z, w_s, b_s[:, :, None], g_v, conv_w)
    yab, gch_s, v_s = _mixer(z, False, e0, e1, ws_s, bs_s[:, :, None], g_v, conv_w, yab_in=yab)

    x = _residual_mm(yab, w_ab_out, a, x, mod_p, mod_s, 2, tm=512, tn=512, name="ab_out", vmem_mib=44)
    conv_p = tail_p.reshape(BATCH, SEQ // MIX_TM, 8, B_WIDTH)[:, -1, 8 - 2:, :]
    conv_s = gch_s.reshape(DEC_BATCH, DEC_SEQ, B_WIDTH)[:, DEC_SEQ - 2:, :]
    return x, conv_p, conv_s, v_s.reshape(DEC_BATCH, DEC_SEQ, A_WIDTH)


def _rope_tables():
    half = QK_ROPE // 2
    inv = ROPE_THETA ** (-jnp.arange(half, dtype=F32) / half)
    pos_p = jnp.arange(TP, dtype=jnp.int32) % SEQ
    pos_s = PAST_LEN + jnp.arange(TS, dtype=jnp.int32) % DEC_SEQ
    pos = jnp.concatenate([pos_p, pos_s])
    ang = pos.astype(F32)[:, None] * inv[None, :]
    cos, sin = jnp.cos(ang), jnp.sin(ang)
    return jnp.concatenate([cos, cos], axis=-1), jnp.concatenate([-sin, sin], axis=-1)


def _swap_halves(w, n_groups):
    k = w.shape[0]
    w3 = w.reshape(k, n_groups, QK_ROPE)
    half = QK_ROPE // 2
    return jnp.concatenate([w3[..., half:], w3[..., :half]], axis=-1).reshape(k, n_groups * QK_ROPE)


def _mla_layer(x, h, m, cache_kv, cache_kr, page_table, c_w_in, c_q_norm_g, c_kv_norm_g, c_w_q_up,
               c_w_kv_up, c_w_out, mod_p, mod_s, cos64, sin64):
    def ep_cq(accs, ex, outs):
        outs[0][...] = _rms(accs[0], ex[0][...]).astype(BF16)

    cq = _mm(h, [c_w_in], ep_cq, tm=512, tn=Q_LORA, n_cols=Q_LORA,
             out_shapes=[jax.ShapeDtypeStruct((T, Q_LORA), BF16)],
             out_specs=[_tile_spec(512, Q_LORA)], name="mla_in_q", vmem_mib=56, w_layer=m,
             extras=[c_q_norm_g[m].reshape(1, Q_LORA)],
             extra_specs=[pl.BlockSpec((1, Q_LORA), lambda j, i: (0, 0))])[0]

    def ep_ckv(accs, ex, outs):
        y = _rms(accs[0], ex[0][...])
        outs[0][...] = y
        outs[1][...] = y.astype(BF16)

    ckv, ckv_bf = _mm(h, [c_w_in], ep_ckv, tm=1024, tn=KV_LORA, n_cols=KV_LORA,
                      out_shapes=[jax.ShapeDtypeStruct((T, KV_LORA), F32),
                                  jax.ShapeDtypeStruct((T, KV_LORA), BF16)],
                      out_specs=[_tile_spec(1024, KV_LORA)] * 2, name="mla_in_kv", vmem_mib=48,
                      w_layer=m, w_col_off=(Q_LORA // KV_LORA,),
                      extras=[c_kv_norm_g[m].reshape(1, KV_LORA)],
                      extra_specs=[pl.BlockSpec((1, KV_LORA), lambda j, i: (0, 0))])

    def ep_rope(accs, ex, outs):
        y = accs[0] * ex[0][...] + accs[1] * ex[1][...]
        for o in outs:
            o[...] = y.astype(o.dtype)

    w_kr = c_w_in[m, :, Q_LORA + KV_LORA:]
    w_kr_sw = _swap_halves(w_kr, 1)
    kr, kr_bf = _mm(h, [w_kr[None], w_kr_sw[None]], ep_rope, tm=1024, tn=QK_ROPE, n_cols=QK_ROPE,
                    out_shapes=[jax.ShapeDtypeStruct((T, QK_ROPE), F32),
                                jax.ShapeDtypeStruct((T, QK_ROPE), BF16)],
                    out_specs=[_tile_spec(1024, QK_ROPE)] * 2, name="mla_in_kr", vmem_mib=40,
                    extras=[cos64, sin64],
                    extra_specs=[pl.BlockSpec((1024, QK_ROPE), lambda j, i: (i, 0))] * 2)

    w_q3 = c_w_q_up[m].reshape(Q_LORA, N_HEADS, QK_NOPE + QK_ROPE)
    w_q_nope = w_q3[:, :, :QK_NOPE].reshape(Q_LORA, N_HEADS * QK_NOPE)
    w_q_rope = w_q3[:, :, QK_NOPE:].reshape(Q_LORA, N_HEADS * QK_ROPE)
    w_q_rope_sw = _swap_halves(w_q_rope, N_HEADS)

    def ep_q_rope(accs, ex, outs):
        y = accs[0] * ex[0][...] + accs[1] * ex[1][...]
        outs[0][...] = y.astype(BF16)

        @pl.when(pl.program_id(1) == TP // 1024)
        def _():
            outs[1][...] = y

    reps = 512 // QK_ROPE
    cos512 = jnp.tile(cos64, (1, reps))
    sin512 = jnp.tile(sin64, (1, reps))
    q_rope, q_rope_s = _mm(cq, [w_q_rope[None], w_q_rope_sw[None]], ep_q_rope, tm=1024, tn=512,
                           n_cols=N_HEADS * QK_ROPE,
                           out_shapes=[jax.ShapeDtypeStruct((T, N_HEADS * QK_ROPE), BF16),
                                       jax.ShapeDtypeStruct((TS, N_HEADS * QK_ROPE), F32)],
                           out_specs=[_tile_spec(1024, 512),
                                      pl.BlockSpec((TS, 512), lambda j, i: (0, j))],
                           name="mla_q_rope", vmem_mib=40, extras=[cos512, sin512],
                           extra_specs=[pl.BlockSpec((1024, 512), lambda j, i: (i, 0))] * 2)

    w_kv3 = c_w_kv_up[m].reshape(KV_LORA, N_HEADS, QK_NOPE + V_HEAD)
    w_uk_t = jnp.transpose(w_kv3[:, :, :QK_NOPE], (1, 2, 0))
    w_uv_bf = jnp.transpose(w_kv3[:, :, QK_NOPE:], (1, 0, 2)).astype(BF16)
    q_lat, q_lat_s = _q_lat(cq, w_q_nope, w_uk_t)

    o = _prompt_attn(q_lat, q_rope, ckv_bf, kr_bf, w_uv_bf)
    cache_kr_t = jnp.swapaxes(cache_kr, 2, 3)
    o_lat_s = _sample_attn(page_table, q_lat_s, q_rope_s, cache_kv, cache_kr_t, m, ckv, kr)
    o = _uv_sample(o_lat_s, w_uv_bf, o)
    x = _residual_mm(o, c_w_out, m, x, mod_p, mod_s, 2, tm=512, tn=512, name="mla_out", vmem_mib=44)
    return x, ckv, kr


def kernel(x_prompt, x_sample, state_conv, cache_kv_latent, cache_k_rope, page_table, c_prompt, c_sample,
           norm_g, w_mod, b_mod, w_ab_in, w_ab_out, a_spatial_w, a_spatial_b, a_v_norm_g, b_conv_w,
           c_w_in, c_q_norm_g, c_kv_norm_g, c_w_q_up, c_w_kv_up, c_w_out,
           ffn_w_gate, ffn_w_up, ffn_w_down, final_norm_g):
    depth = norm_g.shape[0]
    x = jnp.concatenate([x_prompt.reshape(TP, D_MODEL), x_sample.reshape(TS, D_MODEL)], axis=0)
    c_all = jnp.concatenate([c_prompt, c_sample, jnp.zeros((N_SEQ_PAD - N_SEQ, D_MODEL), F32)], axis=0)
    mod = _adaln(c_all, w_mod, b_mod)
    g_all = norm_g.reshape(depth * 2, 1, D_MODEL)
    cos64, sin64 = _rope_tables()

    conv_p, conv_s, chunk_v_s = [], [], []
    kv_p, kr_p, kv_s, kr_s = [], [], [], []
    for li in range(depth):
        mod_p = mod[li, :BATCH].reshape(BATCH, 1, N_MOD * D_MODEL)
        mod_s = jnp.repeat(mod[li, BATCH:N_SEQ], DEC_SEQ, axis=0)
        h = _norm_mod(x, g_all, li * 2, mod_p, mod_s, 0)
        if li % 2 == 0:
            a = li // 2
            x, cp, cs, vs = _ab_layer(x, h, a, state_conv, w_ab_in, w_ab_out, a_spatial_w, a_spatial_b,
                                      a_v_norm_g, b_conv_w, mod_p, mod_s)
            conv_p.append(cp)
            conv_s.append(cs)
            chunk_v_s.append(vs)
        else:
            m = li // 2
            x, ckv, kr = _mla_layer(x, h, m, cache_kv_latent, cache_k_rope, page_table, c_w_in, c_q_norm_g,
                                    c_kv_norm_g, c_w_q_up, c_w_kv_up, c_w_out, mod_p, mod_s, cos64, sin64)
            kv_p.append(ckv[:TP].reshape(BATCH, SEQ, KV_LORA))
            kr_p.append(kr[:TP].reshape(BATCH, SEQ, QK_ROPE))
            kv_s.append(ckv[TP:].reshape(DEC_BATCH, DEC_SEQ, KV_LORA))
            kr_s.append(kr[TP:].reshape(DEC_BATCH, DEC_SEQ, QK_ROPE))
        h = _norm_mod(x, g_all, li * 2 + 1, mod_p, mod_s, 3)
        x = _ffn(x, h, li, ffn_w_gate, ffn_w_up, ffn_w_down, mod_p, mod_s)

    y_prompt = _final_norm(x, final_norm_g, 0, TP, (BATCH, SEQ, D_MODEL))
    y_sample = _final_norm(x, final_norm_g, TP // NORM_TM, TS, (DEC_BATCH, DEC_SEQ, D_MODEL))
    return (y_prompt, y_sample, jnp.stack(conv_p), jnp.stack(conv_s), jnp.stack(chunk_v_s),
            jnp.stack(kv_p), jnp.stack(kr_p), jnp.stack(kv_s), jnp.stack(kr_s))
```

```python
import functools
import math

import jax
import jax.numpy as jnp
from jax import lax
from jax.experimental import pallas as pl
from jax.experimental.pallas import tpu as pltpu

F32 = jnp.float32
BF16 = jnp.bfloat16

D_MODEL = 4096
BATCH = 4
SEQ = 2048
DEC_BATCH = 128
DEC_SEQ = 8
PAST_LEN = 16384
PAGE_SIZE = 128
N_PAGES = PAST_LEN // PAGE_SIZE
CHUNK = 128
A_GROUPS = 8
A_WIDTH = D_MODEL // 2
A_GROUP_DIM = A_WIDTH // A_GROUPS
B_WIDTH = D_MODEL // 2
AB_IN = 2 * A_WIDTH + 3 * B_WIDTH
N_HEADS = 32
QK_NOPE = 128
QK_ROPE = 64
V_HEAD = 128
Q_LORA = 1024
KV_LORA = 512
ROPE_THETA = 10000.0
ATTN_SCALE = 1.0 / math.sqrt(QK_NOPE + QK_ROPE)
D_FF = 11008
N_MOD = 6
EPS = 1e-6

TP = BATCH * SEQ
TS = DEC_BATCH * DEC_SEQ
T = TP + TS
N_SEQ = BATCH + DEC_BATCH
N_SEQ_PAD = 136

MIB = 1024 * 1024
W_CONVERT_ROWS = 512

ARB = "arbitrary"


def _params(n_axes, vmem_mib):
    return pltpu.CompilerParams(dimension_semantics=(ARB,) * n_axes,
                                vmem_limit_bytes=vmem_mib * MIB)


def _dot(a, b):
    return jnp.dot(a, b, preferred_element_type=F32)


def _dot_nt(a, b):
    return lax.dot_general(a, b, (((1,), (1,)), ((), ())), preferred_element_type=F32)


ADALN_TN = 512


def _adaln_body(c_ref, w_ref, b_ref, o_ref):
    c = c_ref[...]
    a = (c * jax.nn.sigmoid(c)).astype(BF16)
    w = w_ref[...].astype(BF16)
    o_ref[...] = _dot(a, w) + b_ref[...]


def _adaln(c_all, w_mod, b_mod):
    depth = w_mod.shape[0]
    n = w_mod.shape[2]
    return pl.pallas_call(
        _adaln_body,
        grid=(depth, n // ADALN_TN),
        in_specs=[
            pl.BlockSpec((N_SEQ_PAD, D_MODEL), lambda l, j: (0, 0)),
            pl.BlockSpec((None, D_MODEL, ADALN_TN), lambda l, j: (l, 0, j)),
            pl.BlockSpec((None, 1, ADALN_TN), lambda l, j: (l, 0, j)),
        ],
        out_specs=pl.BlockSpec((None, N_SEQ_PAD, ADALN_TN), lambda l, j: (l, 0, j)),
        out_shape=jax.ShapeDtypeStruct((depth, N_SEQ_PAD, n), F32),
        compiler_params=_params(2, 40),
        name="adaln",
    )(c_all, w_mod, b_mod.reshape(depth, 1, n))


def _mod_specs(tm, tn, col_blk, ij):
    n_p_tiles = TP // tm
    tiles_per_seq = SEQ // tm

    def p_map(*g):
        i, j = ij(*g)
        return (jnp.minimum(i // tiles_per_seq, BATCH - 1), 0, col_blk(j))

    def s_map(*g):
        i, j = ij(*g)
        return (jnp.maximum(i - n_p_tiles, 0), col_blk(j))

    return [pl.BlockSpec((None, 1, tn), p_map), pl.BlockSpec((tm, tn), s_map)]


def _pick_mod(i, tm, p_ref, s_ref):
    return jnp.where(i < TP // tm, p_ref[...], s_ref[...])


NORM_TM = 256


def _rms(x, g):
    return x * lax.rsqrt(jnp.mean(x * x, axis=-1, keepdims=True) + EPS) * g


def _norm_mod_body(x_ref, g_ref, shp_ref, shs_ref, scp_ref, scs_ref, o_ref):
    i = pl.program_id(0)
    y = _rms(x_ref[...], g_ref[...])
    shift = _pick_mod(i, NORM_TM, shp_ref, shs_ref)
    scale = _pick_mod(i, NORM_TM, scp_ref, scs_ref)
    o_ref[...] = (y * (1.0 + scale) + shift).astype(BF16)


def _norm_mod(x, g_all, g_idx, mod_p, mod_s, shift_chunk):
    ij = lambda i: (i, 0)
    specs = (_mod_specs(NORM_TM, D_MODEL, lambda j: shift_chunk, ij)
             + _mod_specs(NORM_TM, D_MODEL, lambda j: shift_chunk + 1, ij))
    return pl.pallas_call(
        _norm_mod_body,
        grid=(T // NORM_TM,),
        in_specs=[pl.BlockSpec((NORM_TM, D_MODEL), lambda i: (i, 0)),
                  pl.BlockSpec((None, 1, D_MODEL), lambda i: (g_idx, 0, 0))] + specs,
        out_specs=pl.BlockSpec((NORM_TM, D_MODEL), lambda i: (i, 0)),
        out_shape=jax.ShapeDtypeStruct((T, D_MODEL), BF16),
        compiler_params=_params(1, 48),
        name="norm_mod",
    )(x, g_all, mod_p, mod_s, mod_p, mod_s)


def _final_norm_body(x_ref, g_ref, o_ref):
    o_ref[...] = _rms(x_ref[...], g_ref[...])


def _final_norm(x, g, row_blk0, n_rows, out_shape3):
    s = out_shape3[1]
    if s >= NORM_TM:
        per = s // NORM_TM
        o_spec = pl.BlockSpec((None, NORM_TM, D_MODEL), lambda i: (i // per, i % per, 0))
        out = jax.ShapeDtypeStruct(out_shape3, F32)
    else:
        o_spec = pl.BlockSpec((NORM_TM, D_MODEL), lambda i: (i, 0))
        out = jax.ShapeDtypeStruct((n_rows, D_MODEL), F32)
    y = pl.pallas_call(
        _final_norm_body,
        grid=(n_rows // NORM_TM,),
        in_specs=[pl.BlockSpec((NORM_TM, D_MODEL), lambda i: (i + row_blk0, 0)),
                  pl.BlockSpec((1, D_MODEL), lambda i: (0, 0))],
        out_specs=o_spec,
        out_shape=out,
        compiler_params=_params(1, 32),
        name="final_norm",
    )(x, g.reshape(1, D_MODEL))
    return y.reshape(out_shape3)


def _mm(a, ws, epilogue, *, tm, tn, n_cols, out_shapes, out_specs, name, vmem_mib,
        w_layer=0, w_col_off=None, tk=None, kblk=0, extras=(), extra_specs=()):
    m = a.shape[0]
    tk = a.shape[1] if tk is None else tk
    nw, ne, no = len(ws), len(extras), len(out_shapes)
    w_col_off = (0,) * nw if w_col_off is None else w_col_off

    def body(*refs):
        a_ref = refs[0]
        w_refs = refs[1:1 + nw]
        ex = refs[1 + nw:1 + nw + ne]
        outs = refs[1 + nw + ne:1 + nw + ne + no]
        wbf = refs[1 + nw + ne + no:]

        @pl.when(pl.program_id(1) == 0)
        def _():
            for w, wb in zip(w_refs, wbf):
                for r in range(0, tk, W_CONVERT_ROWS):
                    rows = min(W_CONVERT_ROWS, tk - r)
                    wb[r:r + rows, :] = w[r:r + rows, :].astype(BF16)

        av = a_ref[...]
        accs = [_dot(av, wb[...]) for wb in wbf]
        epilogue(accs, ex, outs)

    in_specs = [pl.BlockSpec((tm, tk), lambda j, i: (i, kblk))]
    for off in w_col_off:
        in_specs.append(pl.BlockSpec((None, tk, tn), lambda j, i, off=off: (w_layer, kblk, j + off)))
    in_specs += list(extra_specs)
    return pl.pallas_call(
        body,
        grid=(n_cols // tn, m // tm),
        in_specs=in_specs,
        out_specs=out_specs,
        out_shape=out_shapes,
        scratch_shapes=[pltpu.VMEM((tk, tn), BF16) for _ in range(nw)],
        compiler_params=_params(2, vmem_mib),
        name=name,
    )(a, *ws, *extras)


def _tile_spec(tm, tn, col_off=0):
    return pl.BlockSpec((tm, tn), lambda j, i: (i, j + col_off))


def _residual_mm(a, w, w_layer, x, mod_p, mod_s, gate_chunk, *, tm, tn, name, vmem_mib,
                 tk=None, kblk=0, partial=None):
    ij = lambda j, i: (i, j)
    gate_specs = _mod_specs(tm, tn, lambda j: gate_chunk * (D_MODEL // tn) + j, ij)
    has_partial = partial is not None

    def epilogue(accs, ex, outs):
        i = pl.program_id(1)
        acc = accs[0]
        if has_partial:
            acc = acc + ex[3][...]
        gate = _pick_mod(i, tm, ex[1], ex[2])
        outs[0][...] = ex[0][...] + gate * acc

    extras = [x, mod_p, mod_s] + ([partial] if has_partial else [])
    extra_specs = [_tile_spec(tm, tn)] + gate_specs + ([_tile_spec(tm, tn)] if has_partial else [])
    return _mm(a, [w], epilogue, tm=tm, tn=tn, n_cols=D_MODEL,
               out_shapes=[jax.ShapeDtypeStruct((T, D_MODEL), F32)],
               out_specs=[_tile_spec(tm, tn)], name=name, vmem_mib=vmem_mib,
               w_layer=w_layer, tk=tk, kblk=kblk, extras=extras, extra_specs=extra_specs)[0]


def _partial_mm(a, w, w_layer, *, tm, tn, tk, kblk, name, vmem_mib):
    def epilogue(accs, ex, outs):
        outs[0][...] = accs[0]

    return _mm(a, [w], epilogue, tm=tm, tn=tn, n_cols=D_MODEL,
               out_shapes=[jax.ShapeDtypeStruct((T, D_MODEL), F32)],
               out_specs=[_tile_spec(tm, tn)], name=name, vmem_mib=vmem_mib,
               w_layer=w_layer, tk=tk, kblk=kblk)[0]


MIX_TM = CHUNK


def _layer_norm(x, g):
    mu = jnp.mean(x, axis=-1, keepdims=True)
    xc = x - mu
    return xc * lax.rsqrt(jnp.mean(xc * xc, axis=-1, keepdims=True) + EPS) * g


def _mixer_body(is_prompt, *refs):
    if not is_prompt:
        _mixer_tile(False, *refs)
        return
    i = pl.program_id(0)

    @pl.when(i < TP // MIX_TM)
    def _():
        _mixer_tile(True, *refs)

    @pl.when(i >= TP // MIX_TM)
    def _():
        yab_ref = refs[-2]
        yab_ref[...] = jnp.zeros_like(yab_ref)


def _mixer_tile(is_prompt, u_ref, vg_ref, gb_ref, gc_ref, hi_ref, h1_ref, h2_ref,
                ws_ref, bs_ref, gv_ref, cw_ref, *rest):
    if is_prompt:
        yab_ref, tail_ref = rest
    else:
        _, yab_ref, tail_ref, v_ref = rest
    i = pl.program_id(0)
    v = _layer_norm(vg_ref[...], gv_ref[...])
    if not is_prompt:
        v_ref[...] = v
    vb = v.astype(BF16)
    row = lax.broadcasted_iota(jnp.int32, (MIX_TM, MIX_TM), 0)
    col = lax.broadcasted_iota(jnp.int32, (MIX_TM, MIX_TM), 1)
    mask = col <= row
    if not is_prompt:
        mask = mask & ((col // DEC_SEQ) == (row // DEC_SEQ))
    for g in range(A_GROUPS):
        sl = slice(g * A_GROUP_DIM, (g + 1) * A_GROUP_DIM)
        w = jnp.where(mask, ws_ref[g], 0.0).astype(BF16)
        s = _dot(w, vb[:, sl]) + bs_ref[g]
        yab_ref[:, sl] = (u_ref[:, sl] * s).astype(BF16)

    gch = gc_ref[...] * hi_ref[...]
    pos = lax.broadcasted_iota(jnp.int32, (MIX_TM, B_WIDTH), 0)
    r1 = pltpu.roll(gch, 1, 0)
    r2 = pltpu.roll(gch, 2, 0)
    if is_prompt:
        halo = jnp.where(i % (SEQ // MIX_TM) == 0, 0.0, h1_ref[...] * h2_ref[...])
        p1 = halo[7:8, :]
        p2 = halo[6:7, :]
        x1 = jnp.where(pos == 0, p1, r1)
        x2 = jnp.where(pos == 0, p2, jnp.where(pos == 1, p1, r2))
    else:
        pos = pos % DEC_SEQ
        e0 = h1_ref[...]
        e1 = h2_ref[...]
        x1 = jnp.where(pos == 0, e1, r1)
        x2 = jnp.where(pos == 0, e0, jnp.where(pos == 1, e1, r2))
    yc = x2 * cw_ref[0:1, :] + x1 * cw_ref[1:2, :] + gch * cw_ref[2:3, :]
    yab_ref[:, A_WIDTH:] = (gb_ref[...] * yc).astype(BF16)
    if is_prompt:
        tail_ref[...] = gch[MIX_TM - 8:, :]
    else:
        tail_ref[...] = gch


def _mixer(z, is_prompt, halo_a, halo_b, ws_eff, bs_eff, g_v, conv_w, yab_in=None):
    n_tiles = (TP if is_prompt else TS) // MIX_TM
    blk0 = 0 if is_prompt else TP // MIX_TM
    tile = (lambda i: jnp.minimum(i, n_tiles - 1)) if is_prompt else (lambda i: i)
    zc = lambda c: pl.BlockSpec((MIX_TM, A_WIDTH), lambda i, c=c: (tile(i) + blk0, c))
    if is_prompt:
        rows8 = MIX_TM // 8
        halo_specs = [pl.BlockSpec((8, B_WIDTH), lambda i: (jnp.maximum(tile(i) * rows8 - 1, 0), 3)),
                      pl.BlockSpec((8, B_WIDTH), lambda i: (jnp.maximum(tile(i) * rows8 - 1, 0), 4))]
    else:
        halo_specs = [pl.BlockSpec((MIX_TM, B_WIDTH), lambda i: (i, 0)),
                      pl.BlockSpec((MIX_TM, B_WIDTH), lambda i: (i, 0))]
    in_specs = [zc(0), zc(1), zc(2), zc(3), zc(4)] + halo_specs + [
        pl.BlockSpec((A_GROUPS, MIX_TM, MIX_TM), lambda i: (0, 0, 0)),
        pl.BlockSpec((A_GROUPS, MIX_TM, 1), lambda i: (0, 0, 0)),
        pl.BlockSpec((1, A_WIDTH), lambda i: (0, 0)),
        pl.BlockSpec((3, B_WIDTH), lambda i: (0, 0)),
    ]
    args = [z, z, z, z, z, halo_a, halo_b, ws_eff, bs_eff, g_v, conv_w]
    yab_spec = pl.BlockSpec((MIX_TM, D_MODEL), lambda i: (i + blk0, 0))
    yab_shape = jax.ShapeDtypeStruct((T, D_MODEL), BF16)
    if is_prompt:
        out_shapes = [yab_shape, jax.ShapeDtypeStruct((n_tiles * 8, B_WIDTH), F32)]
        out_specs = [yab_spec, pl.BlockSpec((8, B_WIDTH), lambda i: (tile(i), 0))]
        aliases = {}
    else:
        in_specs.append(pl.BlockSpec(memory_space=pl.ANY))
        args.append(yab_in)
        out_shapes = [yab_shape, jax.ShapeDtypeStruct((TS, B_WIDTH), F32),
                      jax.ShapeDtypeStruct((TS, A_WIDTH), F32)]
        out_specs = [yab_spec, pl.BlockSpec((MIX_TM, B_WIDTH), lambda i: (i, 0)),
                     pl.BlockSpec((MIX_TM, A_WIDTH), lambda i: (i, 0))]
        aliases = {len(args) - 1: 0}
    return pl.pallas_call(
        functools.partial(_mixer_body, is_prompt),
        grid=(T // MIX_TM if is_prompt else n_tiles,),
        in_specs=in_specs,
        out_specs=out_specs,
        out_shape=out_shapes,
        input_output_aliases=aliases,
        compiler_params=_params(1, 40),
        name="mixer_prompt" if is_prompt else "mixer_sample",
    )(*args)


Q_TM = 1024
Q_HEADS = 4
Q_SAMPLE_TILE = TP // Q_TM


def _q_lat_body(cq_ref, wq_ref, wuk_ref, o_ref, os_ref, wq_bf, wuk_bf):
    i = pl.program_id(1)

    @pl.when(i == 0)
    def _():
        wq_bf[...] = wq_ref[...].astype(BF16)
        wuk_bf[...] = wuk_ref[...].astype(BF16)

    qn = _dot(cq_ref[...], wq_bf[...]).astype(BF16)
    for hh in range(Q_HEADS):
        ql = _dot(qn[:, hh * QK_NOPE:(hh + 1) * QK_NOPE], wuk_bf[hh])
        o_ref[:, hh * KV_LORA:(hh + 1) * KV_LORA] = ql.astype(BF16)

        @pl.when(i == Q_SAMPLE_TILE)
        def _():
            os_ref[:, hh * KV_LORA:(hh + 1) * KV_LORA] = ql


def _q_lat(cq, w_q_nope, w_uk_t):
    tn_in = Q_HEADS * QK_NOPE
    tn_out = Q_HEADS * KV_LORA
    return pl.pallas_call(
        _q_lat_body,
        grid=(N_HEADS // Q_HEADS, T // Q_TM),
        in_specs=[pl.BlockSpec((Q_TM, Q_LORA), lambda j, i: (i, 0)),
                  pl.BlockSpec((Q_LORA, tn_in), lambda j, i: (0, j)),
                  pl.BlockSpec((Q_HEADS, QK_NOPE, KV_LORA), lambda j, i: (j, 0, 0))],
        out_specs=[pl.BlockSpec((Q_TM, tn_out), lambda j, i: (i, j)),
                   pl.BlockSpec((TS, tn_out), lambda j, i: (0, j))],
        out_shape=[jax.ShapeDtypeStruct((T, N_HEADS * KV_LORA), BF16),
                   jax.ShapeDtypeStruct((TS, N_HEADS * KV_LORA), F32)],
        scratch_shapes=[pltpu.VMEM((Q_LORA, tn_in), BF16),
                        pltpu.VMEM((Q_HEADS, QK_NOPE, KV_LORA), BF16)],
        compiler_params=_params(2, 48),
        name="mla_q_lat",
    )(cq, w_q_nope, w_uk_t)


def _uv_sample_body(a_ref, w_ref, _, o_ref):
    o_ref[...] = _dot(a_ref[...].astype(BF16), w_ref[...]).astype(BF16)


def _uv_sample(o_lat_s, w_uv_bf, o_in):
    return pl.pallas_call(
        _uv_sample_body,
        grid=(N_HEADS,),
        in_specs=[pl.BlockSpec((TS, KV_LORA), lambda h: (0, h)),
                  pl.BlockSpec((None, KV_LORA, V_HEAD), lambda h: (h, 0, 0)),
                  pl.BlockSpec(memory_space=pl.ANY)],
        out_specs=pl.BlockSpec((TS, V_HEAD), lambda h: (TP // TS, h)),
        out_shape=jax.ShapeDtypeStruct((T, N_HEADS * V_HEAD), BF16),
        input_output_aliases={2: 0},
        compiler_params=_params(1, 32),
        name="mla_uv_sample",
    )(o_lat_s, w_uv_bf, o_in)


PQ = 128
PK = 256
P_ROWS = PQ * N_HEADS
PR = 512
PR_HEADS = PR // PQ


def _prompt_attend(nk, qb, qs, qrs, kv_ref, kr_ref, wuv_ref, o3):
    def scores(c):
        r0 = pl.multiple_of(c * PR, PR)
        return _dot_nt(qs[pl.ds(r0, PR), :], kv_ref[0:nk, :]) + _dot_nt(qrs[pl.ds(r0, PR), :], kr_ref[0:nk, :])

    def finish(c, s):
        kv = kv_ref[0:nk, :]
        tok = qb * PQ + lax.broadcasted_iota(jnp.int32, (PR, PK), 0) % PQ
        key = (nk - PK) + lax.broadcasted_iota(jnp.int32, (PR, PK), 1)
        tail = jnp.where(key <= tok, s[:, nk - PK:], -jnp.inf)
        m = jnp.max(tail, axis=-1, keepdims=True)
        if nk > PK:
            head = s[:, :nk - PK]
            m = jnp.maximum(m, jnp.max(head, axis=-1, keepdims=True))
            p_head = jnp.exp((head - m) * ATTN_SCALE)
            p_tail = jnp.exp((tail - m) * ATTN_SCALE)
            l = jnp.sum(p_head, axis=-1, keepdims=True) + jnp.sum(p_tail, axis=-1, keepdims=True)
            p = jnp.concatenate([p_head.astype(BF16), p_tail.astype(BF16)], axis=1)
        else:
            p_tail = jnp.exp((tail - m) * ATTN_SCALE)
            l = jnp.sum(p_tail, axis=-1, keepdims=True)
            p = p_tail.astype(BF16)
        o_lat = (_dot(p, kv) * (1.0 / l)).astype(BF16)
        for hh in range(PR_HEADS):
            h = c * PR_HEADS + hh
            o3[h] = _dot(o_lat[hh * PQ:(hh + 1) * PQ, :], wuv_ref[h]).astype(BF16)

    def pair(cp, carry):
        s_a = scores(2 * cp)
        s_b = scores(2 * cp + 1)
        finish(2 * cp, s_a)
        finish(2 * cp + 1, s_b)
        return carry

    lax.fori_loop(0, P_ROWS // PR // 2, pair, 0)


def _prompt_attn_body(ql_ref, qr_ref, kv_ref, kr_ref, wuv_ref, o_ref, qs, qrs, o3):
    t = pl.program_id(0)
    qb = t % (SEQ // PQ)
    is_prompt = t < TP // PQ

    @pl.when(is_prompt)
    def _():
        for h in range(N_HEADS):
            qs[h * PQ:(h + 1) * PQ, :] = ql_ref[:, h * KV_LORA:(h + 1) * KV_LORA]
            qrs[h * PQ:(h + 1) * PQ, :] = qr_ref[:, h * QK_ROPE:(h + 1) * QK_ROPE]

    for v in range(SEQ // PK):
        @pl.when(is_prompt & ((qb * PQ) // PK == v))
        def _():
            _prompt_attend((v + 1) * PK, qb, qs, qrs, kv_ref, kr_ref, wuv_ref, o3)

    @pl.when(is_prompt)
    def _():
        for h in range(N_HEADS):
            o_ref[:, h * V_HEAD:(h + 1) * V_HEAD] = o3[h]

    @pl.when(jnp.logical_not(is_prompt))
    def _():
        o_ref[...] = jnp.zeros_like(o_ref)


def _prompt_attn(q_lat, q_rope, ckv_bf, kr_bf, w_uv_bf):
    nq = SEQ // PQ
    last = TP // PQ - 1
    q_map = lambda t: (jnp.minimum(t, last), 0)
    k_map = lambda t: (jnp.minimum(t, last) // nq, 0)
    return pl.pallas_call(
        _prompt_attn_body,
        grid=(T // PQ,),
        in_specs=[pl.BlockSpec((PQ, N_HEADS * KV_LORA), q_map),
                  pl.BlockSpec((PQ, N_HEADS * QK_ROPE), q_map),
                  pl.BlockSpec((SEQ, KV_LORA), k_map),
                  pl.BlockSpec((SEQ, QK_ROPE), k_map),
                  pl.BlockSpec((N_HEADS, KV_LORA, V_HEAD), lambda t: (0, 0, 0))],
        out_specs=pl.BlockSpec((PQ, N_HEADS * V_HEAD), lambda t: (t, 0)),
        out_shape=jax.ShapeDtypeStruct((T, N_HEADS * V_HEAD), BF16),
        scratch_shapes=[pltpu.VMEM((P_ROWS, KV_LORA), BF16), pltpu.VMEM((P_ROWS, QK_ROPE), BF16),
                        pltpu.VMEM((N_HEADS, PQ, V_HEAD), BF16)],
        compiler_params=_params(1, 56),
        name="prompt_attn",
    )(q_lat, q_rope, ckv_bf, kr_bf, w_uv_bf)


S_PAGES = 32
S_CHUNKS = N_PAGES // S_PAGES
S_ROWS = DEC_SEQ * N_HEADS
S_SLOTS = 3


S_SUB = 4
S_SUBS = S_PAGES // S_SUB


def _online_softmax(s, v, m, l, acc):
    m_new = jnp.maximum(m, jnp.max(s, axis=-1, keepdims=True))
    alpha = jnp.exp((m - m_new) * ATTN_SCALE)
    p = jnp.exp((s - m_new) * ATTN_SCALE)
    l = alpha * l + jnp.sum(p, axis=-1, keepdims=True)
    acc = alpha * acc + _dot(p.astype(BF16), v)
    return m_new, l, acc


def _page_copies(layer, pt_ref, cache_kv, cache_kr_t, kvbuf, krbuf, sem, step, slot, known_pages):
    copies = []
    for k in range(S_PAGES):
        page = pt_ref[step * S_PAGES + k] if known_pages else 0
        copies.append(pltpu.make_async_copy(cache_kv.at[layer, page], kvbuf.at[slot, k], sem.at[slot]))
        copies.append(pltpu.make_async_copy(cache_kr_t.at[layer, page], krbuf.at[slot, k], sem.at[slot]))
    return copies


def _sample_attn_body(layer, pt_ref, ql_ref, qr_ref, kvn_ref, krn_ref, cache_kv, cache_kr_t, o_ref,
                      kvbuf, krbuf, sem, qs, qrs, m_ref, l_ref, acc_ref):
    c = pl.program_id(1)
    step = pl.program_id(0) * S_CHUNKS + c
    slot = lax.rem(step, S_SLOTS)
    copies = functools.partial(_page_copies, layer, pt_ref, cache_kv, cache_kr_t, kvbuf, krbuf, sem)

    @pl.when(step == 0)
    def _():
        for ahead in range(S_SLOTS - 1):
            for cp in copies(ahead, ahead, True):
                cp.start()

    for cp in copies(step, slot, False):
        cp.wait()

    @pl.when(c == 0)
    def _():
        for h in range(N_HEADS):
            qs[h * DEC_SEQ:(h + 1) * DEC_SEQ, :] = ql_ref[:, h * KV_LORA:(h + 1) * KV_LORA]
            qrs[h * DEC_SEQ:(h + 1) * DEC_SEQ, :] = qr_ref[:, h * QK_ROPE:(h + 1) * QK_ROPE]
        m_ref[...] = jnp.full_like(m_ref, -jnp.inf)
        l_ref[...] = jnp.zeros_like(l_ref)
        acc_ref[...] = jnp.zeros_like(acc_ref)

    q = qs[...].astype(BF16)
    qr = qrs[...].astype(BF16)
    m, l, acc = m_ref[...], l_ref[...], acc_ref[...]
    def scores(sb):
        pages = range(sb * S_SUB, (sb + 1) * S_SUB)
        kv = jnp.concatenate([kvbuf[slot, k].astype(BF16) for k in pages], axis=0)
        kr_t = jnp.concatenate([krbuf[slot, k].astype(BF16) for k in pages], axis=1)
        return _dot_nt(q, kv) + _dot(qr, kr_t), kv

    pending = scores(0)
    for sb in range(S_SUBS):
        following = scores(sb + 1) if sb + 1 < S_SUBS else None
        m, l, acc = _online_softmax(*pending, m, l, acc)
        pending = following
    m_ref[...] = m
    l_ref[...] = l
    acc_ref[...] = acc

    @pl.when(c == S_CHUNKS - 1)
    def _():
        pad = PAGE_SIZE - DEC_SEQ
        kvn = jnp.concatenate([kvn_ref[...], jnp.zeros((pad, KV_LORA), F32)], axis=0).astype(BF16)
        krn = jnp.concatenate([krn_ref[...], jnp.zeros((pad, QK_ROPE), F32)], axis=0).astype(BF16)
        sn = _dot_nt(q, kvn) + _dot_nt(qr, krn)
        tok = lax.broadcasted_iota(jnp.int32, (S_ROWS, PAGE_SIZE), 0) % DEC_SEQ
        key = lax.broadcasted_iota(jnp.int32, (S_ROWS, PAGE_SIZE), 1)
        sn = jnp.where(key <= tok, sn, -jnp.inf)
        _, l2, acc2 = _online_softmax(sn, kvn, m, l, acc)
        o = acc2 * (1.0 / l2)
        for h in range(N_HEADS):
            o_ref[:, h * KV_LORA:(h + 1) * KV_LORA] = o[h * DEC_SEQ:(h + 1) * DEC_SEQ, :]

    nxt = step + (S_SLOTS - 1)

    @pl.when(nxt < DEC_BATCH * S_CHUNKS)
    def _():
        for cp in copies(nxt, lax.rem(nxt, S_SLOTS), True):
            cp.start()


def _sample_attn(page_table, q_lat_s, q_rope_s, cache_kv, cache_kr_t, layer, ckv_f32, kr_f32):
    q_map = lambda b, c, pt: (b, 0)
    new_map = lambda b, c, pt: (TP // DEC_SEQ + b, 0)
    grid_spec = pltpu.PrefetchScalarGridSpec(
        num_scalar_prefetch=1,
        grid=(DEC_BATCH, S_CHUNKS),
        in_specs=[pl.BlockSpec((DEC_SEQ, N_HEADS * KV_LORA), q_map),
                  pl.BlockSpec((DEC_SEQ, N_HEADS * QK_ROPE), q_map),
                  pl.BlockSpec((DEC_SEQ, KV_LORA), new_map),
                  pl.BlockSpec((DEC_SEQ, QK_ROPE), new_map),
                  pl.BlockSpec(memory_space=pl.ANY),
                  pl.BlockSpec(memory_space=pl.ANY)],
        out_specs=pl.BlockSpec((DEC_SEQ, N_HEADS * KV_LORA), q_map),
        scratch_shapes=[pltpu.VMEM((S_SLOTS, S_PAGES, PAGE_SIZE, KV_LORA), F32),
                        pltpu.VMEM((S_SLOTS, S_PAGES, QK_ROPE, PAGE_SIZE), F32),
                        pltpu.SemaphoreType.DMA((S_SLOTS,)),
                        pltpu.VMEM((S_ROWS, KV_LORA), F32), pltpu.VMEM((S_ROWS, QK_ROPE), F32),
                        pltpu.VMEM((S_ROWS, 1), F32), pltpu.VMEM((S_ROWS, 1), F32),
                        pltpu.VMEM((S_ROWS, KV_LORA), F32)],
    )
    return pl.pallas_call(
        functools.partial(_sample_attn_body, layer),
        grid_spec=grid_spec,
        out_shape=jax.ShapeDtypeStruct((TS, N_HEADS * KV_LORA), F32),
        compiler_params=_params(2, 48),
        name="sample_attn",
    )(page_table.reshape(-1), q_lat_s, q_rope_s, ckv_f32, kr_f32, cache_kv, cache_kr_t)


def _ffn(x, h, li, w_gate, w_up, w_down, mod_p, mod_s):
    def ep_glu(accs, ex, outs):
        g, u = accs
        outs[0][...] = (g * jax.nn.sigmoid(g) * u).astype(BF16)

    tn = 256
    act = _mm(h, [w_gate, w_up], ep_glu, tm=1536, tn=tn, n_cols=D_FF,
              out_shapes=[jax.ShapeDtypeStruct((T, D_FF), BF16)],
              out_specs=[_tile_spec(1536, tn)], name="ffn_gate_up", vmem_mib=56, w_layer=li)[0]
    half = D_FF // 2
    part = _partial_mm(act, w_down, li, tm=512, tn=512, tk=half, kblk=0, name="ffn_down_a", vmem_mib=48)
    return _residual_mm(act, w_down, li, x, mod_p, mod_s, 5, tm=512, tn=512, tk=half, kblk=1,
                        partial=part, name="ffn_down_b", vmem_mib=52)


def _ab_layer(x, h, a, state_conv, w_ab_in, w_ab_out, a_spatial_w, a_spatial_b, a_v_norm_g, b_conv_w,
              mod_p, mod_s):
    def ep_in(accs, ex, outs):
        j = pl.program_id(0)
        z = accs[0]

        @pl.when(j < (2 * A_WIDTH) // 512)
        def _():
            outs[0][...] = jax.nn.gelu(z)

        @pl.when(j >= (2 * A_WIDTH) // 512)
        def _():
            outs[0][...] = z

    z = _mm(h, [w_ab_in], ep_in, tm=1024, tn=512, n_cols=AB_IN,
            out_shapes=[jax.ShapeDtypeStruct((T, AB_IN), F32)],
            out_specs=[_tile_spec(1024, 512)], name="ab_in", vmem_mib=48, w_layer=a)[0]

    w_s = a_spatial_w[a]
    b_s = a_spatial_b[a]
    reps = MIX_TM // DEC_SEQ
    ws_s = jnp.tile(w_s[:, :DEC_SEQ, :DEC_SEQ], (1, reps, reps))
    bs_s = jnp.tile(b_s[:, :DEC_SEQ], (1, reps))
    g_v = a_v_norm_g[a].reshape(1, A_WIDTH)
    conv_w = b_conv_w[a]
    e0 = jnp.repeat(state_conv[a, :, 0, :], DEC_SEQ, axis=0)
    e1 = jnp.repeat(state_conv[a, :, 1, :], DEC_SEQ, axis=0)

    yab, tail_p = _mixer(z, True, z, z, w_s, b_s[:, :, None], g_v, conv_w)
    yab, gch_s, v_s = _mixer(z, False, e0, e1, ws_s, bs_s[:, :, None], g_v, conv_w, yab_in=yab)

    x = _residual_mm(yab, w_ab_out, a, x, mod_p, mod_s, 2, tm=1024, tn=512, name="ab_out", vmem_mib=56)
    conv_p = tail_p.reshape(BATCH, SEQ // MIX_TM, 8, B_WIDTH)[:, -1, 8 - 2:, :]
    conv_s = gch_s.reshape(DEC_BATCH, DEC_SEQ, B_WIDTH)[:, DEC_SEQ - 2:, :]
    return x, conv_p, conv_s, v_s.reshape(DEC_BATCH, DEC_SEQ, A_WIDTH)


def _rope_tables():
    half = QK_ROPE // 2
    inv = ROPE_THETA ** (-jnp.arange(half, dtype=F32) / half)
    pos_p = jnp.arange(TP, dtype=jnp.int32) % SEQ
    pos_s = PAST_LEN + jnp.arange(TS, dtype=jnp.int32) % DEC_SEQ
    pos = jnp.concatenate([pos_p, pos_s])
    ang = pos.astype(F32)[:, None] * inv[None, :]
    cos, sin = jnp.cos(ang), jnp.sin(ang)
    return jnp.concatenate([cos, cos], axis=-1), jnp.concatenate([-sin, sin], axis=-1)


def _swap_halves(w, n_groups):
    k = w.shape[0]
    w3 = w.reshape(k, n_groups, QK_ROPE)
    half = QK_ROPE // 2
    return jnp.concatenate([w3[..., half:], w3[..., :half]], axis=-1).reshape(k, n_groups * QK_ROPE)


def _mla_layer(x, h, m, cache_kv, cache_kr, page_table, c_w_in, c_q_norm_g, c_kv_norm_g, c_w_q_up,
               c_w_kv_up, c_w_out, mod_p, mod_s, cos64, sin64):
    def ep_cq(accs, ex, outs):
        outs[0][...] = _rms(accs[0], ex[0][...]).astype(BF16)

    cq = _mm(h, [c_w_in], ep_cq, tm=512, tn=Q_LORA, n_cols=Q_LORA,
             out_shapes=[jax.ShapeDtypeStruct((T, Q_LORA), BF16)],
             out_specs=[_tile_spec(512, Q_LORA)], name="mla_in_q", vmem_mib=56, w_layer=m,
             extras=[c_q_norm_g[m].reshape(1, Q_LORA)],
             extra_specs=[pl.BlockSpec((1, Q_LORA), lambda j, i: (0, 0))])[0]

    def ep_ckv(accs, ex, outs):
        y = _rms(accs[0], ex[0][...])
        outs[0][...] = y
        outs[1][...] = y.astype(BF16)

    ckv, ckv_bf = _mm(h, [c_w_in], ep_ckv, tm=1024, tn=KV_LORA, n_cols=KV_LORA,
                      out_shapes=[jax.ShapeDtypeStruct((T, KV_LORA), F32),
                                  jax.ShapeDtypeStruct((T, KV_LORA), BF16)],
                      out_specs=[_tile_spec(1024, KV_LORA)] * 2, name="mla_in_kv", vmem_mib=48,
                      w_layer=m, w_col_off=(Q_LORA // KV_LORA,),
                      extras=[c_kv_norm_g[m].reshape(1, KV_LORA)],
                      extra_specs=[pl.BlockSpec((1, KV_LORA), lambda j, i: (0, 0))])

    def ep_rope(accs, ex, outs):
        y = accs[0] * ex[0][...] + accs[1] * ex[1][...]
        for o in outs:
            o[...] = y.astype(o.dtype)

    w_kr = c_w_in[m, :, Q_LORA + KV_LORA:]
    w_kr_sw = _swap_halves(w_kr, 1)
    kr, kr_bf = _mm(h, [w_kr[None], w_kr_sw[None]], ep_rope, tm=1024, tn=QK_ROPE, n_cols=QK_ROPE,
                    out_shapes=[jax.ShapeDtypeStruct((T, QK_ROPE), F32),
                                jax.ShapeDtypeStruct((T, QK_ROPE), BF16)],
                    out_specs=[_tile_spec(1024, QK_ROPE)] * 2, name="mla_in_kr", vmem_mib=40,
                    extras=[cos64, sin64],
                    extra_specs=[pl.BlockSpec((1024, QK_ROPE), lambda j, i: (i, 0))] * 2)

    w_q3 = c_w_q_up[m].reshape(Q_LORA, N_HEADS, QK_NOPE + QK_ROPE)
    w_q_nope = w_q3[:, :, :QK_NOPE].reshape(Q_LORA, N_HEADS * QK_NOPE)
    w_q_rope = w_q3[:, :, QK_NOPE:].reshape(Q_LORA, N_HEADS * QK_ROPE)
    w_q_rope_sw = _swap_halves(w_q_rope, N_HEADS)

    def ep_q_rope(accs, ex, outs):
        y = accs[0] * ex[0][...] + accs[1] * ex[1][...]
        outs[0][...] = y.astype(BF16)

        @pl.when(pl.program_id(1) == TP // 1024)
        def _():
            outs[1][...] = y

    reps = 512 // QK_ROPE
    cos512 = jnp.tile(cos64, (1, reps))
    sin512 = jnp.tile(sin64, (1, reps))
    q_rope, q_rope_s = _mm(cq, [w_q_rope[None], w_q_rope_sw[None]], ep_q_rope, tm=1024, tn=512,
                           n_cols=N_HEADS * QK_ROPE,
                           out_shapes=[jax.ShapeDtypeStruct((T, N_HEADS * QK_ROPE), BF16),
                                       jax.ShapeDtypeStruct((TS, N_HEADS * QK_ROPE), F32)],
                           out_specs=[_tile_spec(1024, 512),
                                      pl.BlockSpec((TS, 512), lambda j, i: (0, j))],
                           name="mla_q_rope", vmem_mib=40, extras=[cos512, sin512],
                           extra_specs=[pl.BlockSpec((1024, 512), lambda j, i: (i, 0))] * 2)

    w_kv3 = c_w_kv_up[m].reshape(KV_LORA, N_HEADS, QK_NOPE + V_HEAD)
    w_uk_t = jnp.transpose(w_kv3[:, :, :QK_NOPE], (1, 2, 0))
    w_uv_bf = jnp.transpose(w_kv3[:, :, QK_NOPE:], (1, 0, 2)).astype(BF16)
    q_lat, q_lat_s = _q_lat(cq, w_q_nope, w_uk_t)

    o = _prompt_attn(q_lat, q_rope, ckv_bf, kr_bf, w_uv_bf)
    cache_kr_t = jnp.swapaxes(cache_kr, 2, 3)
    o_lat_s = _sample_attn(page_table, q_lat_s, q_rope_s, cache_kv, cache_kr_t, m, ckv, kr)
    o = _uv_sample(o_lat_s, w_uv_bf, o)
    x = _residual_mm(o, c_w_out, m, x, mod_p, mod_s, 2, tm=1024, tn=512, name="mla_out", vmem_mib=56)
    return x, ckv, kr


def kernel(x_prompt, x_sample, state_conv, cache_kv_latent, cache_k_rope, page_table, c_prompt, c_sample,
           norm_g, w_mod, b_mod, w_ab_in, w_ab_out, a_spatial_w, a_spatial_b, a_v_norm_g, b_conv_w,
           c_w_in, c_q_norm_g, c_kv_norm_g, c_w_q_up, c_w_kv_up, c_w_out,
           ffn_w_gate, ffn_w_up, ffn_w_down, final_norm_g):
    depth = norm_g.shape[0]
    x = jnp.concatenate([x_prompt.reshape(TP, D_MODEL), x_sample.reshape(TS, D_MODEL)], axis=0)
    c_all = jnp.concatenate([c_prompt, c_sample, jnp.zeros((N_SEQ_PAD - N_SEQ, D_MODEL), F32)], axis=0)
    mod = _adaln(c_all, w_mod, b_mod)
    g_all = norm_g.reshape(depth * 2, 1, D_MODEL)
    cos64, sin64 = _rope_tables()

    conv_p, conv_s, chunk_v_s = [], [], []
    kv_p, kr_p, kv_s, kr_s = [], [], [], []
    for li in range(depth):
        mod_p = mod[li, :BATCH].reshape(BATCH, 1, N_MOD * D_MODEL)
        mod_s = jnp.repeat(mod[li, BATCH:N_SEQ], DEC_SEQ, axis=0)
        h = _norm_mod(x, g_all, li * 2, mod_p, mod_s, 0)
        if li % 2 == 0:
            a = li // 2
            x, cp, cs, vs = _ab_layer(x, h, a, state_conv, w_ab_in, w_ab_out, a_spatial_w, a_spatial_b,
                                      a_v_norm_g, b_conv_w, mod_p, mod_s)
            conv_p.append(cp)
            conv_s.append(cs)
            chunk_v_s.append(vs)
        else:
            m = li // 2
            x, ckv, kr = _mla_layer(x, h, m, cache_kv_latent, cache_k_rope, page_table, c_w_in, c_q_norm_g,
                                    c_kv_norm_g, c_w_q_up, c_w_kv_up, c_w_out, mod_p, mod_s, cos64, sin64)
            kv_p.append(ckv[:TP].reshape(BATCH, SEQ, KV_LORA))
            kr_p.append(kr[:TP].reshape(BATCH, SEQ, QK_ROPE))
            kv_s.append(ckv[TP:].reshape(DEC_BATCH, DEC_SEQ, KV_LORA))
            kr_s.append(kr[TP:].reshape(DEC_BATCH, DEC_SEQ, QK_ROPE))
        h = _norm_mod(x, g_all, li * 2 + 1, mod_p, mod_s, 3)
        x = _ffn(x, h, li, ffn_w_gate, ffn_w_up, ffn_w_down, mod_p, mod_s)

    y_prompt = _final_norm(x, final_norm_g, 0, TP, (BATCH, SEQ, D_MODEL))
    y_sample = _final_norm(x, final_norm_g, TP // NORM_TM, TS, (DEC_BATCH, DEC_SEQ, D_MODEL))
    return (y_prompt, y_sample, jnp.stack(conv_p), jnp.stack(conv_s), jnp.stack(chunk_v_s),
            jnp.stack(kv_p), jnp.stack(kr_p), jnp.stack(kv_s), jnp.stack(kr_s))
```

```python
import functools
import math

import jax
import jax.numpy as jnp
from jax import lax
from jax.experimental import pallas as pl
from jax.experimental.pallas import tpu as pltpu

F32 = jnp.float32
BF16 = jnp.bfloat16

D_MODEL = 4096
BATCH = 4
SEQ = 2048
DEC_BATCH = 128
DEC_SEQ = 8
PAST_LEN = 16384
PAGE_SIZE = 128
N_PAGES = PAST_LEN // PAGE_SIZE
CHUNK = 128
A_GROUPS = 8
A_WIDTH = D_MODEL // 2
A_GROUP_DIM = A_WIDTH // A_GROUPS
B_WIDTH = D_MODEL // 2
AB_IN = 2 * A_WIDTH + 3 * B_WIDTH
N_HEADS = 32
QK_NOPE = 128
QK_ROPE = 64
V_HEAD = 128
Q_LORA = 1024
KV_LORA = 512
ROPE_THETA = 10000.0
ATTN_SCALE = 1.0 / math.sqrt(QK_NOPE + QK_ROPE)
D_FF = 11008
N_MOD = 6
EPS = 1e-6

TP = BATCH * SEQ
TS = DEC_BATCH * DEC_SEQ
T = TP + TS
N_SEQ = BATCH + DEC_BATCH
N_SEQ_PAD = 136

MIB = 1024 * 1024
W_CONVERT_ROWS = 512

ARB = "arbitrary"


def _params(n_axes, vmem_mib):
    return pltpu.CompilerParams(dimension_semantics=(ARB,) * n_axes,
                                vmem_limit_bytes=vmem_mib * MIB)


def _dot(a, b):
    return jnp.dot(a, b, preferred_element_type=F32)


def _dot_nt(a, b):
    return lax.dot_general(a, b, (((1,), (1,)), ((), ())), preferred_element_type=F32)


ADALN_TN = 512


def _adaln_body(c_ref, w_ref, b_ref, o_ref):
    c = c_ref[...]
    a = (c * jax.nn.sigmoid(c)).astype(BF16)
    w = w_ref[...].astype(BF16)
    o_ref[...] = _dot(a, w) + b_ref[...]


def _adaln(c_all, w_mod, b_mod):
    depth = w_mod.shape[0]
    n = w_mod.shape[2]
    return pl.pallas_call(
        _adaln_body,
        grid=(depth, n // ADALN_TN),
        in_specs=[
            pl.BlockSpec((N_SEQ_PAD, D_MODEL), lambda l, j: (0, 0)),
            pl.BlockSpec((None, D_MODEL, ADALN_TN), lambda l, j: (l, 0, j)),
            pl.BlockSpec((None, 1, ADALN_TN), lambda l, j: (l, 0, j)),
        ],
        out_specs=pl.BlockSpec((None, N_SEQ_PAD, ADALN_TN), lambda l, j: (l, 0, j)),
        out_shape=jax.ShapeDtypeStruct((depth, N_SEQ_PAD, n), F32),
        compiler_params=_params(2, 40),
        name="adaln",
    )(c_all, w_mod, b_mod.reshape(depth, 1, n))


def _mod_specs(tm, tn, col_blk, ij):
    n_p_tiles = TP // tm
    tiles_per_seq = SEQ // tm

    def p_map(*g):
        i, j = ij(*g)
        return (jnp.minimum(i // tiles_per_seq, BATCH - 1), 0, col_blk(j))

    def s_map(*g):
        i, j = ij(*g)
        return (jnp.maximum(i - n_p_tiles, 0), col_blk(j))

    return [pl.BlockSpec((None, 1, tn), p_map), pl.BlockSpec((tm, tn), s_map)]


def _pick_mod(i, tm, p_ref, s_ref):
    return jnp.where(i < TP // tm, p_ref[...], s_ref[...])


NORM_TM = 256


def _rms(x, g):
    return x * lax.rsqrt(jnp.mean(x * x, axis=-1, keepdims=True) + EPS) * g


def _norm_mod_body(x_ref, g_ref, shp_ref, shs_ref, scp_ref, scs_ref, o_ref):
    i = pl.program_id(0)
    y = _rms(x_ref[...], g_ref[...])

    @pl.when(i < TP // NORM_TM)
    def _():
        o_ref[...] = (y * (1.0 + scp_ref[...]) + shp_ref[...]).astype(BF16)

    @pl.when(i >= TP // NORM_TM)
    def _():
        o_ref[...] = (y * (1.0 + scs_ref[...]) + shs_ref[...]).astype(BF16)


def _norm_mod(x, g_all, g_idx, mod_p, mod_s, shift_chunk):
    ij = lambda i: (i, 0)
    specs = (_mod_specs(NORM_TM, D_MODEL, lambda j: shift_chunk, ij)
             + _mod_specs(NORM_TM, D_MODEL, lambda j: shift_chunk + 1, ij))
    return pl.pallas_call(
        _norm_mod_body,
        grid=(T // NORM_TM,),
        in_specs=[pl.BlockSpec((NORM_TM, D_MODEL), lambda i: (i, 0)),
                  pl.BlockSpec((None, 1, D_MODEL), lambda i: (g_idx, 0, 0))] + specs,
        out_specs=pl.BlockSpec((NORM_TM, D_MODEL), lambda i: (i, 0)),
        out_shape=jax.ShapeDtypeStruct((T, D_MODEL), BF16),
        compiler_params=_params(1, 48),
        name="norm_mod",
    )(x, g_all, mod_p, mod_s, mod_p, mod_s)


def _final_norm_body(x_ref, g_ref, o_ref):
    o_ref[...] = _rms(x_ref[...], g_ref[...])


def _final_norm(x, g, row_blk0, n_rows, out_shape3):
    s = out_shape3[1]
    if s >= NORM_TM:
        per = s // NORM_TM
        o_spec = pl.BlockSpec((None, NORM_TM, D_MODEL), lambda i: (i // per, i % per, 0))
        out = jax.ShapeDtypeStruct(out_shape3, F32)
    else:
        o_spec = pl.BlockSpec((NORM_TM, D_MODEL), lambda i: (i, 0))
        out = jax.ShapeDtypeStruct((n_rows, D_MODEL), F32)
    y = pl.pallas_call(
        _final_norm_body,
        grid=(n_rows // NORM_TM,),
        in_specs=[pl.BlockSpec((NORM_TM, D_MODEL), lambda i: (i + row_blk0, 0)),
                  pl.BlockSpec((1, D_MODEL), lambda i: (0, 0))],
        out_specs=o_spec,
        out_shape=out,
        compiler_params=_params(1, 32),
        name="final_norm",
    )(x, g.reshape(1, D_MODEL))
    return y.reshape(out_shape3)


def _mm(a, ws, epilogue, *, tm, tn, n_cols, out_shapes, out_specs, name, vmem_mib,
        w_layer=0, w_col_off=None, tk=None, kblk=0, extras=(), extra_specs=()):
    m = a.shape[0]
    tk = a.shape[1] if tk is None else tk
    nw, ne, no = len(ws), len(extras), len(out_shapes)
    w_col_off = (0,) * nw if w_col_off is None else w_col_off

    def body(*refs):
        a_ref = refs[0]
        w_refs = refs[1:1 + nw]
        ex = refs[1 + nw:1 + nw + ne]
        outs = refs[1 + nw + ne:1 + nw + ne + no]
        wbf = refs[1 + nw + ne + no:]

        @pl.when(pl.program_id(1) == 0)
        def _():
            for w, wb in zip(w_refs, wbf):
                for r in range(0, tk, W_CONVERT_ROWS):
                    rows = min(W_CONVERT_ROWS, tk - r)
                    wb[r:r + rows, :] = w[r:r + rows, :].astype(BF16)

        av = a_ref[...]
        accs = [_dot(av, wb[...]) for wb in wbf]
        epilogue(accs, ex, outs)

    in_specs = [pl.BlockSpec((tm, tk), lambda j, i: (i, kblk))]
    for off in w_col_off:
        in_specs.append(pl.BlockSpec((None, tk, tn), lambda j, i, off=off: (w_layer, kblk, j + off)))
    in_specs += list(extra_specs)
    return pl.pallas_call(
        body,
        grid=(n_cols // tn, m // tm),
        in_specs=in_specs,
        out_specs=out_specs,
        out_shape=out_shapes,
        scratch_shapes=[pltpu.VMEM((tk, tn), BF16) for _ in range(nw)],
        compiler_params=_params(2, vmem_mib),
        name=name,
    )(a, *ws, *extras)


def _tile_spec(tm, tn, col_off=0):
    return pl.BlockSpec((tm, tn), lambda j, i: (i, j + col_off))


def _residual_mm(a, w, w_layer, x, mod_p, mod_s, gate_chunk, *, tm, tn, name, vmem_mib,
                 tk=None, kblk=0, partial=None):
    ij = lambda j, i: (i, j)
    gate_specs = _mod_specs(tm, tn, lambda j: gate_chunk * (D_MODEL // tn) + j, ij)
    has_partial = partial is not None

    def epilogue(accs, ex, outs):
        i = pl.program_id(1)
        acc = accs[0]
        if has_partial:
            acc = acc + ex[3][...]
        gate = _pick_mod(i, tm, ex[1], ex[2])
        outs[0][...] = ex[0][...] + gate * acc

    extras = [x, mod_p, mod_s] + ([partial] if has_partial else [])
    extra_specs = [_tile_spec(tm, tn)] + gate_specs + ([_tile_spec(tm, tn)] if has_partial else [])
    return _mm(a, [w], epilogue, tm=tm, tn=tn, n_cols=D_MODEL,
               out_shapes=[jax.ShapeDtypeStruct((T, D_MODEL), F32)],
               out_specs=[_tile_spec(tm, tn)], name=name, vmem_mib=vmem_mib,
               w_layer=w_layer, tk=tk, kblk=kblk, extras=extras, extra_specs=extra_specs)[0]


def _partial_mm(a, w, w_layer, *, tm, tn, tk, kblk, name, vmem_mib):
    def epilogue(accs, ex, outs):
        outs[0][...] = accs[0]

    return _mm(a, [w], epilogue, tm=tm, tn=tn, n_cols=D_MODEL,
               out_shapes=[jax.ShapeDtypeStruct((T, D_MODEL), F32)],
               out_specs=[_tile_spec(tm, tn)], name=name, vmem_mib=vmem_mib,
               w_layer=w_layer, tk=tk, kblk=kblk)[0]


MIX_TM = CHUNK


def _layer_norm(x, g):
    mu = jnp.mean(x, axis=-1, keepdims=True)
    xc = x - mu
    return xc * lax.rsqrt(jnp.mean(xc * xc, axis=-1, keepdims=True) + EPS) * g


def _mixer_body(is_prompt, *refs):
    if not is_prompt:
        _mixer_tile(False, *refs)
        return
    i = pl.program_id(0)

    @pl.when(i < TP // MIX_TM)
    def _():
        _mixer_tile(True, *refs)

    @pl.when(i >= TP // MIX_TM)
    def _():
        yab_ref = refs[-2]
        yab_ref[...] = jnp.zeros_like(yab_ref)


def _mixer_tile(is_prompt, u_ref, vg_ref, gb_ref, gc_ref, hi_ref, h1_ref, h2_ref,
                ws_ref, bs_ref, gv_ref, cw_ref, *rest):
    if is_prompt:
        yab_ref, tail_ref = rest
    else:
        _, yab_ref, tail_ref, v_ref = rest
    i = pl.program_id(0)
    v = _layer_norm(vg_ref[...], gv_ref[...])
    if not is_prompt:
        v_ref[...] = v
    vb = v.astype(BF16)
    row = lax.broadcasted_iota(jnp.int32, (MIX_TM, MIX_TM), 0)
    col = lax.broadcasted_iota(jnp.int32, (MIX_TM, MIX_TM), 1)
    mask = col <= row
    if not is_prompt:
        mask = mask & ((col // DEC_SEQ) == (row // DEC_SEQ))
    for g in range(A_GROUPS):
        sl = slice(g * A_GROUP_DIM, (g + 1) * A_GROUP_DIM)
        w = jnp.where(mask, ws_ref[g], 0.0).astype(BF16)
        s = _dot(w, vb[:, sl]) + bs_ref[g]
        yab_ref[:, sl] = (u_ref[:, sl] * s).astype(BF16)

    gch = gc_ref[...] * hi_ref[...]
    pos = lax.broadcasted_iota(jnp.int32, (MIX_TM, B_WIDTH), 0)
    r1 = pltpu.roll(gch, 1, 0)
    r2 = pltpu.roll(gch, 2, 0)
    if is_prompt:
        halo = jnp.where(i % (SEQ // MIX_TM) == 0, 0.0, h1_ref[...] * h2_ref[...])
        p1 = halo[7:8, :]
        p2 = halo[6:7, :]
        x1 = jnp.where(pos == 0, p1, r1)
        x2 = jnp.where(pos == 0, p2, jnp.where(pos == 1, p1, r2))
    else:
        pos = pos % DEC_SEQ
        e0 = h1_ref[...]
        e1 = h2_ref[...]
        x1 = jnp.where(pos == 0, e1, r1)
        x2 = jnp.where(pos == 0, e0, jnp.where(pos == 1, e1, r2))
    yc = x2 * cw_ref[0:1, :] + x1 * cw_ref[1:2, :] + gch * cw_ref[2:3, :]
    yab_ref[:, A_WIDTH:] = (gb_ref[...] * yc).astype(BF16)
    if is_prompt:
        tail_ref[...] = gch[MIX_TM - 8:, :]
    else:
        tail_ref[...] = gch


def _mixer(z, is_prompt, halo_a, halo_b, ws_eff, bs_eff, g_v, conv_w, yab_in=None):
    n_tiles = (TP if is_prompt else TS) // MIX_TM
    blk0 = 0 if is_prompt else TP // MIX_TM
    tile = (lambda i: jnp.minimum(i, n_tiles - 1)) if is_prompt else (lambda i: i)
    zc = lambda c: pl.BlockSpec((MIX_TM, A_WIDTH), lambda i, c=c: (tile(i) + blk0, c))
    if is_prompt:
        rows8 = MIX_TM // 8
        halo_specs = [pl.BlockSpec((8, B_WIDTH), lambda i: (jnp.maximum(tile(i) * rows8 - 1, 0), 3)),
                      pl.BlockSpec((8, B_WIDTH), lambda i: (jnp.maximum(tile(i) * rows8 - 1, 0), 4))]
    else:
        halo_specs = [pl.BlockSpec((MIX_TM, B_WIDTH), lambda i: (i, 0)),
                      pl.BlockSpec((MIX_TM, B_WIDTH), lambda i: (i, 0))]
    in_specs = [zc(0), zc(1), zc(2), zc(3), zc(4)] + halo_specs + [
        pl.BlockSpec((A_GROUPS, MIX_TM, MIX_TM), lambda i: (0, 0, 0)),
        pl.BlockSpec((A_GROUPS, MIX_TM, 1), lambda i: (0, 0, 0)),
        pl.BlockSpec((1, A_WIDTH), lambda i: (0, 0)),
        pl.BlockSpec((3, B_WIDTH), lambda i: (0, 0)),
    ]
    args = [z, z, z, z, z, halo_a, halo_b, ws_eff, bs_eff, g_v, conv_w]
    yab_spec = pl.BlockSpec((MIX_TM, D_MODEL), lambda i: (i + blk0, 0))
    yab_shape = jax.ShapeDtypeStruct((T, D_MODEL), BF16)
    if is_prompt:
        out_shapes = [yab_shape, jax.ShapeDtypeStruct((n_tiles * 8, B_WIDTH), F32)]
        out_specs = [yab_spec, pl.BlockSpec((8, B_WIDTH), lambda i: (tile(i), 0))]
        aliases = {}
    else:
        in_specs.append(pl.BlockSpec(memory_space=pl.ANY))
        args.append(yab_in)
        out_shapes = [yab_shape, jax.ShapeDtypeStruct((TS, B_WIDTH), F32),
                      jax.ShapeDtypeStruct((TS, A_WIDTH), F32)]
        out_specs = [yab_spec, pl.BlockSpec((MIX_TM, B_WIDTH), lambda i: (i, 0)),
                     pl.BlockSpec((MIX_TM, A_WIDTH), lambda i: (i, 0))]
        aliases = {len(args) - 1: 0}
    return pl.pallas_call(
        functools.partial(_mixer_body, is_prompt),
        grid=(T // MIX_TM if is_prompt else n_tiles,),
        in_specs=in_specs,
        out_specs=out_specs,
        out_shape=out_shapes,
        input_output_aliases=aliases,
        compiler_params=_params(1, 40),
        name="mixer_prompt" if is_prompt else "mixer_sample",
    )(*args)


Q_TM = 1024
Q_HEADS = 4
Q_SAMPLE_TILE = TP // Q_TM


def _q_lat_body(cq_ref, wq_ref, wuk_ref, o_ref, os_ref, wq_bf, wuk_bf):
    i = pl.program_id(1)

    @pl.when(i == 0)
    def _():
        wq_bf[...] = wq_ref[...].astype(BF16)
        wuk_bf[...] = wuk_ref[...].astype(BF16)

    qn = _dot(cq_ref[...], wq_bf[...]).astype(BF16)
    for hh in range(Q_HEADS):
        ql = _dot(qn[:, hh * QK_NOPE:(hh + 1) * QK_NOPE], wuk_bf[hh])
        o_ref[:, hh * KV_LORA:(hh + 1) * KV_LORA] = ql.astype(BF16)

        @pl.when(i == Q_SAMPLE_TILE)
        def _():
            os_ref[:, hh * KV_LORA:(hh + 1) * KV_LORA] = ql


def _q_lat(cq, w_q_nope, w_uk_t):
    tn_in = Q_HEADS * QK_NOPE
    tn_out = Q_HEADS * KV_LORA
    return pl.pallas_call(
        _q_lat_body,
        grid=(N_HEADS // Q_HEADS, T // Q_TM),
        in_specs=[pl.BlockSpec((Q_TM, Q_LORA), lambda j, i: (i, 0)),
                  pl.BlockSpec((Q_LORA, tn_in), lambda j, i: (0, j)),
                  pl.BlockSpec((Q_HEADS, QK_NOPE, KV_LORA), lambda j, i: (j, 0, 0))],
        out_specs=[pl.BlockSpec((Q_TM, tn_out), lambda j, i: (i, j)),
                   pl.BlockSpec((TS, tn_out), lambda j, i: (0, j))],
        out_shape=[jax.ShapeDtypeStruct((T, N_HEADS * KV_LORA), BF16),
                   jax.ShapeDtypeStruct((TS, N_HEADS * KV_LORA), F32)],
        scratch_shapes=[pltpu.VMEM((Q_LORA, tn_in), BF16),
                        pltpu.VMEM((Q_HEADS, QK_NOPE, KV_LORA), BF16)],
        compiler_params=_params(2, 48),
        name="mla_q_lat",
    )(cq, w_q_nope, w_uk_t)


def _uv_sample_body(a_ref, w_ref, _, o_ref):
    o_ref[...] = _dot(a_ref[...].astype(BF16), w_ref[...]).astype(BF16)


def _uv_sample(o_lat_s, w_uv_bf, o_in):
    return pl.pallas_call(
        _uv_sample_body,
        grid=(N_HEADS,),
        in_specs=[pl.BlockSpec((TS, KV_LORA), lambda h: (0, h)),
                  pl.BlockSpec((None, KV_LORA, V_HEAD), lambda h: (h, 0, 0)),
                  pl.BlockSpec(memory_space=pl.ANY)],
        out_specs=pl.BlockSpec((TS, V_HEAD), lambda h: (TP // TS, h)),
        out_shape=jax.ShapeDtypeStruct((T, N_HEADS * V_HEAD), BF16),
        input_output_aliases={2: 0},
        compiler_params=_params(1, 32),
        name="mla_uv_sample",
    )(o_lat_s, w_uv_bf, o_in)


PQ = 128
PK = 256
P_ROWS = PQ * N_HEADS
PR = 512
PR_HEADS = PR // PQ


def _prompt_attend(nk, qb, qs, qrs, kv_ref, kr_ref, wuv_ref, o3):
    def scores(c):
        r0 = pl.multiple_of(c * PR, PR)
        return _dot_nt(qs[pl.ds(r0, PR), :], kv_ref[0:nk, :]) + _dot_nt(qrs[pl.ds(r0, PR), :], kr_ref[0:nk, :])

    def finish(c, s):
        kv = kv_ref[0:nk, :]
        tok = qb * PQ + lax.broadcasted_iota(jnp.int32, (PR, PK), 0) % PQ
        key = (nk - PK) + lax.broadcasted_iota(jnp.int32, (PR, PK), 1)
        tail = jnp.where(key <= tok, s[:, nk - PK:], -jnp.inf)
        m = jnp.max(tail, axis=-1, keepdims=True)
        if nk > PK:
            head = s[:, :nk - PK]
            m = jnp.maximum(m, jnp.max(head, axis=-1, keepdims=True))
            p_head = jnp.exp((head - m) * ATTN_SCALE)
            p_tail = jnp.exp((tail - m) * ATTN_SCALE)
            l = jnp.sum(p_head, axis=-1, keepdims=True) + jnp.sum(p_tail, axis=-1, keepdims=True)
            p = jnp.concatenate([p_head.astype(BF16), p_tail.astype(BF16)], axis=1)
        else:
            p_tail = jnp.exp((tail - m) * ATTN_SCALE)
            l = jnp.sum(p_tail, axis=-1, keepdims=True)
            p = p_tail.astype(BF16)
        o_lat = (_dot(p, kv) * (1.0 / l)).astype(BF16)
        for hh in range(PR_HEADS):
            h = c * PR_HEADS + hh
            o3[h] = _dot(o_lat[hh * PQ:(hh + 1) * PQ, :], wuv_ref[h]).astype(BF16)

    def pair(cp, carry):
        s_a = scores(2 * cp)
        s_b = scores(2 * cp + 1)
        finish(2 * cp, s_a)
        finish(2 * cp + 1, s_b)
        return carry

    lax.fori_loop(0, P_ROWS // PR // 2, pair, 0)


def _prompt_attn_body(ql_ref, qr_ref, kv_ref, kr_ref, wuv_ref, o_ref, qs, qrs, o3):
    t = pl.program_id(0)
    qb = t % (SEQ // PQ)
    is_prompt = t < TP // PQ

    @pl.when(is_prompt)
    def _():
        for h in range(N_HEADS):
            qs[h * PQ:(h + 1) * PQ, :] = ql_ref[:, h * KV_LORA:(h + 1) * KV_LORA]
            qrs[h * PQ:(h + 1) * PQ, :] = qr_ref[:, h * QK_ROPE:(h + 1) * QK_ROPE]

    for v in range(SEQ // PK):
        @pl.when(is_prompt & ((qb * PQ) // PK == v))
        def _():
            _prompt_attend((v + 1) * PK, qb, qs, qrs, kv_ref, kr_ref, wuv_ref, o3)

    @pl.when(is_prompt)
    def _():
        for h in range(N_HEADS):
            o_ref[:, h * V_HEAD:(h + 1) * V_HEAD] = o3[h]

    @pl.when(jnp.logical_not(is_prompt))
    def _():
        o_ref[...] = jnp.zeros_like(o_ref)


def _prompt_attn(q_lat, q_rope, ckv_bf, kr_bf, w_uv_bf):
    nq = SEQ // PQ
    last = TP // PQ - 1
    q_map = lambda t: (jnp.minimum(t, last), 0)
    k_map = lambda t: (jnp.minimum(t, last) // nq, 0)
    return pl.pallas_call(
        _prompt_attn_body,
        grid=(T // PQ,),
        in_specs=[pl.BlockSpec((PQ, N_HEADS * KV_LORA), q_map),
                  pl.BlockSpec((PQ, N_HEADS * QK_ROPE), q_map),
                  pl.BlockSpec((SEQ, KV_LORA), k_map),
                  pl.BlockSpec((SEQ, QK_ROPE), k_map),
                  pl.BlockSpec((N_HEADS, KV_LORA, V_HEAD), lambda t: (0, 0, 0))],
        out_specs=pl.BlockSpec((PQ, N_HEADS * V_HEAD), lambda t: (t, 0)),
        out_shape=jax.ShapeDtypeStruct((T, N_HEADS * V_HEAD), BF16),
        scratch_shapes=[pltpu.VMEM((P_ROWS, KV_LORA), BF16), pltpu.VMEM((P_ROWS, QK_ROPE), BF16),
                        pltpu.VMEM((N_HEADS, PQ, V_HEAD), BF16)],
        compiler_params=_params(1, 56),
        name="prompt_attn",
    )(q_lat, q_rope, ckv_bf, kr_bf, w_uv_bf)


S_PAGES = 32
S_CHUNKS = N_PAGES // S_PAGES
S_ROWS = DEC_SEQ * N_HEADS
S_SLOTS = 3


S_SUB = 8
S_SUBS = S_PAGES // S_SUB


def _online_softmax(s, v, m, l, acc):
    m_new = jnp.maximum(m, jnp.max(s, axis=-1, keepdims=True))
    alpha = jnp.exp((m - m_new) * ATTN_SCALE)
    p = jnp.exp((s - m_new) * ATTN_SCALE)
    l = alpha * l + jnp.sum(p, axis=-1, keepdims=True)
    acc = alpha * acc + _dot(p.astype(BF16), v)
    return m_new, l, acc


def _page_copies(layer, pt_ref, cache_kv, cache_kr_t, kvbuf, krbuf, sem, step, slot, known_pages):
    copies = []
    for k in range(S_PAGES):
        page = pt_ref[step * S_PAGES + k] if known_pages else 0
        copies.append(pltpu.make_async_copy(cache_kv.at[layer, page], kvbuf.at[slot, k], sem.at[slot]))
        copies.append(pltpu.make_async_copy(cache_kr_t.at[layer, page], krbuf.at[slot, k], sem.at[slot]))
    return copies


def _sample_attn_body(layer, pt_ref, ql_ref, qr_ref, kvn_ref, krn_ref, cache_kv, cache_kr_t, o_ref,
                      kvbuf, krbuf, sem, qs, qrs, m_ref, l_ref, acc_ref):
    c = pl.program_id(1)
    step = pl.program_id(0) * S_CHUNKS + c
    slot = lax.rem(step, S_SLOTS)
    copies = functools.partial(_page_copies, layer, pt_ref, cache_kv, cache_kr_t, kvbuf, krbuf, sem)

    @pl.when(step == 0)
    def _():
        for ahead in range(S_SLOTS - 1):
            for cp in copies(ahead, ahead, True):
                cp.start()

    for cp in copies(step, slot, False):
        cp.wait()

    @pl.when(c == 0)
    def _():
        for h in range(N_HEADS):
            qs[h * DEC_SEQ:(h + 1) * DEC_SEQ, :] = ql_ref[:, h * KV_LORA:(h + 1) * KV_LORA]
            qrs[h * DEC_SEQ:(h + 1) * DEC_SEQ, :] = qr_ref[:, h * QK_ROPE:(h + 1) * QK_ROPE]
        m_ref[...] = jnp.full_like(m_ref, -jnp.inf)
        l_ref[...] = jnp.zeros_like(l_ref)
        acc_ref[...] = jnp.zeros_like(acc_ref)

    q = qs[...].astype(BF16)
    qr = qrs[...].astype(BF16)
    m, l, acc = m_ref[...], l_ref[...], acc_ref[...]
    def scores(sb):
        pages = range(sb * S_SUB, (sb + 1) * S_SUB)
        kv = jnp.concatenate([kvbuf[slot, k].astype(BF16) for k in pages], axis=0)
        kr_t = jnp.concatenate([krbuf[slot, k].astype(BF16) for k in pages], axis=1)
        return _dot_nt(q, kv) + _dot(qr, kr_t), kv

    pending = scores(0)
    for sb in range(S_SUBS):
        following = scores(sb + 1) if sb + 1 < S_SUBS else None
        m, l, acc = _online_softmax(*pending, m, l, acc)
        pending = following
    m_ref[...] = m
    l_ref[...] = l
    acc_ref[...] = acc

    @pl.when(c == S_CHUNKS - 1)
    def _():
        pad = PAGE_SIZE - DEC_SEQ
        kvn = jnp.concatenate([kvn_ref[...], jnp.zeros((pad, KV_LORA), F32)], axis=0).astype(BF16)
        krn = jnp.concatenate([krn_ref[...], jnp.zeros((pad, QK_ROPE), F32)], axis=0).astype(BF16)
        sn = _dot_nt(q, kvn) + _dot_nt(qr, krn)
        tok = lax.broadcasted_iota(jnp.int32, (S_ROWS, PAGE_SIZE), 0) % DEC_SEQ
        key = lax.broadcasted_iota(jnp.int32, (S_ROWS, PAGE_SIZE), 1)
        sn = jnp.where(key <= tok, sn, -jnp.inf)
        _, l2, acc2 = _online_softmax(sn, kvn, m, l, acc)
        o = acc2 * (1.0 / l2)
        for h in range(N_HEADS):
            o_ref[:, h * KV_LORA:(h + 1) * KV_LORA] = o[h * DEC_SEQ:(h + 1) * DEC_SEQ, :]

    nxt = step + (S_SLOTS - 1)

    @pl.when(nxt < DEC_BATCH * S_CHUNKS)
    def _():
        for cp in copies(nxt, lax.rem(nxt, S_SLOTS), True):
            cp.start()


def _sample_attn(page_table, q_lat_s, q_rope_s, cache_kv, cache_kr_t, layer, ckv_f32, kr_f32):
    q_map = lambda b, c, pt: (b, 0)
    new_map = lambda b, c, pt: (TP // DEC_SEQ + b, 0)
    grid_spec = pltpu.PrefetchScalarGridSpec(
        num_scalar_prefetch=1,
        grid=(DEC_BATCH, S_CHUNKS),
        in_specs=[pl.BlockSpec((DEC_SEQ, N_HEADS * KV_LORA), q_map),
                  pl.BlockSpec((DEC_SEQ, N_HEADS * QK_ROPE), q_map),
                  pl.BlockSpec((DEC_SEQ, KV_LORA), new_map),
                  pl.BlockSpec((DEC_SEQ, QK_ROPE), new_map),
                  pl.BlockSpec(memory_space=pl.ANY),
                  pl.BlockSpec(memory_space=pl.ANY)],
        out_specs=pl.BlockSpec((DEC_SEQ, N_HEADS * KV_LORA), q_map),
        scratch_shapes=[pltpu.VMEM((S_SLOTS, S_PAGES, PAGE_SIZE, KV_LORA), F32),
                        pltpu.VMEM((S_SLOTS, S_PAGES, QK_ROPE, PAGE_SIZE), F32),
                        pltpu.SemaphoreType.DMA((S_SLOTS,)),
                        pltpu.VMEM((S_ROWS, KV_LORA), F32), pltpu.VMEM((S_ROWS, QK_ROPE), F32),
                        pltpu.VMEM((S_ROWS, 1), F32), pltpu.VMEM((S_ROWS, 1), F32),
                        pltpu.VMEM((S_ROWS, KV_LORA), F32)],
    )
    return pl.pallas_call(
        functools.partial(_sample_attn_body, layer),
        grid_spec=grid_spec,
        out_shape=jax.ShapeDtypeStruct((TS, N_HEADS * KV_LORA), F32),
        compiler_params=_params(2, 48),
        name="sample_attn",
    )(page_table.reshape(-1), q_lat_s, q_rope_s, ckv_f32, kr_f32, cache_kv, cache_kr_t)


def _ffn(x, h, li, w_gate, w_up, w_down, mod_p, mod_s):
    def ep_glu(accs, ex, outs):
        g, u = accs
        outs[0][...] = (g * jax.nn.sigmoid(g) * u).astype(BF16)

    tn = 256
    act = _mm(h, [w_gate, w_up], ep_glu, tm=1536, tn=tn, n_cols=D_FF,
              out_shapes=[jax.ShapeDtypeStruct((T, D_FF), BF16)],
              out_specs=[_tile_spec(1536, tn)], name="ffn_gate_up", vmem_mib=56, w_layer=li)[0]
    half = D_FF // 2
    part = _partial_mm(act, w_down, li, tm=512, tn=512, tk=half, kblk=0, name="ffn_down_a", vmem_mib=48)
    return _residual_mm(act, w_down, li, x, mod_p, mod_s, 5, tm=512, tn=512, tk=half, kblk=1,
                        partial=part, name="ffn_down_b", vmem_mib=52)


def _ab_layer(x, h, a, state_conv, w_ab_in, w_ab_out, a_spatial_w, a_spatial_b, a_v_norm_g, b_conv_w,
              mod_p, mod_s):
    def ep_in(accs, ex, outs):
        j = pl.program_id(0)
        z = accs[0]

        @pl.when(j < (2 * A_WIDTH) // 512)
        def _():
            outs[0][...] = jax.nn.gelu(z)

        @pl.when(j >= (2 * A_WIDTH) // 512)
        def _():
            outs[0][...] = z

    z = _mm(h, [w_ab_in], ep_in, tm=1024, tn=512, n_cols=AB_IN,
            out_shapes=[jax.ShapeDtypeStruct((T, AB_IN), F32)],
            out_specs=[_tile_spec(1024, 512)], name="ab_in", vmem_mib=48, w_layer=a)[0]

    w_s = a_spatial_w[a]
    b_s = a_spatial_b[a]
    reps = MIX_TM // DEC_SEQ
    ws_s = jnp.tile(w_s[:, :DEC_SEQ, :DEC_SEQ], (1, reps, reps))
    bs_s = jnp.tile(b_s[:, :DEC_SEQ], (1, reps))
    g_v = a_v_norm_g[a].reshape(1, A_WIDTH)
    conv_w = b_conv_w[a]
    e0 = jnp.repeat(state_conv[a, :, 0, :], DEC_SEQ, axis=0)
    e1 = jnp.repeat(state_conv[a, :, 1, :], DEC_SEQ, axis=0)

    yab, tail_p = _mixer(z, True, z, z, w_s, b_s[:, :, None], g_v, conv_w)
    yab, gch_s, v_s = _mixer(z, False, e0, e1, ws_s, bs_s[:, :, None], g_v, conv_w, yab_in=yab)

    x = _residual_mm(yab, w_ab_out, a, x, mod_p, mod_s, 2, tm=1024, tn=512, name="ab_out", vmem_mib=56)
    conv_p = tail_p.reshape(BATCH, SEQ // MIX_TM, 8, B_WIDTH)[:, -1, 8 - 2:, :]
    conv_s = gch_s.reshape(DEC_BATCH, DEC_SEQ, B_WIDTH)[:, DEC_SEQ - 2:, :]
    return x, conv_p, conv_s, v_s.reshape(DEC_BATCH, DEC_SEQ, A_WIDTH)


def _rope_tables():
    half = QK_ROPE // 2
    inv = ROPE_THETA ** (-jnp.arange(half, dtype=F32) / half)
    pos_p = jnp.arange(TP, dtype=jnp.int32) % SEQ
    pos_s = PAST_LEN + jnp.arange(TS, dtype=jnp.int32) % DEC_SEQ
    pos = jnp.concatenate([pos_p, pos_s])
    ang = pos.astype(F32)[:, None] * inv[None, :]
    cos, sin = jnp.cos(ang), jnp.sin(ang)
    return jnp.concatenate([cos, cos], axis=-1), jnp.concatenate([-sin, sin], axis=-1)


def _swap_halves(w, n_groups):
    k = w.shape[0]
    w3 = w.reshape(k, n_groups, QK_ROPE)
    half = QK_ROPE // 2
    return jnp.concatenate([w3[..., half:], w3[..., :half]], axis=-1).reshape(k, n_groups * QK_ROPE)


def _mla_layer(x, h, m, cache_kv, cache_kr, page_table, c_w_in, c_q_norm_g, c_kv_norm_g, c_w_q_up,
               c_w_kv_up, c_w_out, mod_p, mod_s, cos64, sin64):
    def ep_cq(accs, ex, outs):
        outs[0][...] = _rms(accs[0], ex[0][...]).astype(BF16)

    cq = _mm(h, [c_w_in], ep_cq, tm=512, tn=Q_LORA, n_cols=Q_LORA,
             out_shapes=[jax.ShapeDtypeStruct((T, Q_LORA), BF16)],
             out_specs=[_tile_spec(512, Q_LORA)], name="mla_in_q", vmem_mib=56, w_layer=m,
             extras=[c_q_norm_g[m].reshape(1, Q_LORA)],
             extra_specs=[pl.BlockSpec((1, Q_LORA), lambda j, i: (0, 0))])[0]

    def ep_ckv(accs, ex, outs):
        y = _rms(accs[0], ex[0][...])
        outs[0][...] = y
        outs[1][...] = y.astype(BF16)

    ckv, ckv_bf = _mm(h, [c_w_in], ep_ckv, tm=1024, tn=KV_LORA, n_cols=KV_LORA,
                      out_shapes=[jax.ShapeDtypeStruct((T, KV_LORA), F32),
                                  jax.ShapeDtypeStruct((T, KV_LORA), BF16)],
                      out_specs=[_tile_spec(1024, KV_LORA)] * 2, name="mla_in_kv", vmem_mib=48,
                      w_layer=m, w_col_off=(Q_LORA // KV_LORA,),
                      extras=[c_kv_norm_g[m].reshape(1, KV_LORA)],
                      extra_specs=[pl.BlockSpec((1, KV_LORA), lambda j, i: (0, 0))])

    def ep_rope(accs, ex, outs):
        y = accs[0] * ex[0][...] + accs[1] * ex[1][...]
        for o in outs:
            o[...] = y.astype(o.dtype)

    w_kr = c_w_in[m, :, Q_LORA + KV_LORA:]
    w_kr_sw = _swap_halves(w_kr, 1)
    kr, kr_bf = _mm(h, [w_kr[None], w_kr_sw[None]], ep_rope, tm=1024, tn=QK_ROPE, n_cols=QK_ROPE,
                    out_shapes=[jax.ShapeDtypeStruct((T, QK_ROPE), F32),
                                jax.ShapeDtypeStruct((T, QK_ROPE), BF16)],
                    out_specs=[_tile_spec(1024, QK_ROPE)] * 2, name="mla_in_kr", vmem_mib=40,
                    extras=[cos64, sin64],
                    extra_specs=[pl.BlockSpec((1024, QK_ROPE), lambda j, i: (i, 0))] * 2)

    w_q3 = c_w_q_up[m].reshape(Q_LORA, N_HEADS, QK_NOPE + QK_ROPE)
    w_q_nope = w_q3[:, :, :QK_NOPE].reshape(Q_LORA, N_HEADS * QK_NOPE)
    w_q_rope = w_q3[:, :, QK_NOPE:].reshape(Q_LORA, N_HEADS * QK_ROPE)
    w_q_rope_sw = _swap_halves(w_q_rope, N_HEADS)

    def ep_q_rope(accs, ex, outs):
        y = accs[0] * ex[0][...] + accs[1] * ex[1][...]
        outs[0][...] = y.astype(BF16)

        @pl.when(pl.program_id(1) == TP // 1024)
        def _():
            outs[1][...] = y

    reps = 512 // QK_ROPE
    cos512 = jnp.tile(cos64, (1, reps))
    sin512 = jnp.tile(sin64, (1, reps))
    q_rope, q_rope_s = _mm(cq, [w_q_rope[None], w_q_rope_sw[None]], ep_q_rope, tm=1024, tn=512,
                           n_cols=N_HEADS * QK_ROPE,
                           out_shapes=[jax.ShapeDtypeStruct((T, N_HEADS * QK_ROPE), BF16),
                                       jax.ShapeDtypeStruct((TS, N_HEADS * QK_ROPE), F32)],
                           out_specs=[_tile_spec(1024, 512),
                                      pl.BlockSpec((TS, 512), lambda j, i: (0, j))],
                           name="mla_q_rope", vmem_mib=40, extras=[cos512, sin512],
                           extra_specs=[pl.BlockSpec((1024, 512), lambda j, i: (i, 0))] * 2)

    w_kv3 = c_w_kv_up[m].reshape(KV_LORA, N_HEADS, QK_NOPE + V_HEAD)
    w_uk_t = jnp.transpose(w_kv3[:, :, :QK_NOPE], (1, 2, 0))
    w_uv_bf = jnp.transpose(w_kv3[:, :, QK_NOPE:], (1, 0, 2)).astype(BF16)
    q_lat, q_lat_s = _q_lat(cq, w_q_nope, w_uk_t)

    o = _prompt_attn(q_lat, q_rope, ckv_bf, kr_bf, w_uv_bf)
    cache_kr_t = jnp.swapaxes(cache_kr, 2, 3)
    o_lat_s = _sample_attn(page_table, q_lat_s, q_rope_s, cache_kv, cache_kr_t, m, ckv, kr)
    o = _uv_sample(o_lat_s, w_uv_bf, o)
    x = _residual_mm(o, c_w_out, m, x, mod_p, mod_s, 2, tm=1024, tn=512, name="mla_out", vmem_mib=56)
    return x, ckv, kr


def kernel(x_prompt, x_sample, state_conv, cache_kv_latent, cache_k_rope, page_table, c_prompt, c_sample,
           norm_g, w_mod, b_mod, w_ab_in, w_ab_out, a_spatial_w, a_spatial_b, a_v_norm_g, b_conv_w,
           c_w_in, c_q_norm_g, c_kv_norm_g, c_w_q_up, c_w_kv_up, c_w_out,
           ffn_w_gate, ffn_w_up, ffn_w_down, final_norm_g):
    depth = norm_g.shape[0]
    x = jnp.concatenate([x_prompt.reshape(TP, D_MODEL), x_sample.reshape(TS, D_MODEL)], axis=0)
    c_all = jnp.concatenate([c_prompt, c_sample, jnp.zeros((N_SEQ_PAD - N_SEQ, D_MODEL), F32)], axis=0)
    mod = _adaln(c_all, w_mod, b_mod)
    g_all = norm_g.reshape(depth * 2, 1, D_MODEL)
    cos64, sin64 = _rope_tables()

    conv_p, conv_s, chunk_v_s = [], [], []
    kv_p, kr_p, kv_s, kr_s = [], [], [], []
    for li in range(depth):
        mod_p = mod[li, :BATCH].reshape(BATCH, 1, N_MOD * D_MODEL)
        mod_s = jnp.repeat(mod[li, BATCH:N_SEQ], DEC_SEQ, axis=0)
        h = _norm_mod(x, g_all, li * 2, mod_p, mod_s, 0)
        if li % 2 == 0:
            a = li // 2
            x, cp, cs, vs = _ab_layer(x, h, a, state_conv, w_ab_in, w_ab_out, a_spatial_w, a_spatial_b,
                                      a_v_norm_g, b_conv_w, mod_p, mod_s)
            conv_p.append(cp)
            conv_s.append(cs)
            chunk_v_s.append(vs)
        else:
            m = li // 2
            x, ckv, kr = _mla_layer(x, h, m, cache_kv_latent, cache_k_rope, page_table, c_w_in, c_q_norm_g,
                                    c_kv_norm_g, c_w_q_up, c_w_kv_up, c_w_out, mod_p, mod_s, cos64, sin64)
            kv_p.append(ckv[:TP].reshape(BATCH, SEQ, KV_LORA))
            kr_p.append(kr[:TP].reshape(BATCH, SEQ, QK_ROPE))
            kv_s.append(ckv[TP:].reshape(DEC_BATCH, DEC_SEQ, KV_LORA))
            kr_s.append(kr[TP:].reshape(DEC_BATCH, DEC_SEQ, QK_ROPE))
        h = _norm_mod(x, g_all, li * 2 + 1, mod_p, mod_s, 3)
        x = _ffn(x, h, li, ffn_w_gate, ffn_w_up, ffn_w_down, mod_p, mod_s)

    y_prompt = _final_norm(x, final_norm_g, 0, TP, (BATCH, SEQ, D_MODEL))
    y_sample = _final_norm(x, final_norm_g, TP // NORM_TM, TS, (DEC_BATCH, DEC_SEQ, D_MODEL))
    return (y_prompt, y_sample, jnp.stack(conv_p), jnp.stack(conv_s), jnp.stack(chunk_v_s),
            jnp.stack(kv_p), jnp.stack(kr_p), jnp.stack(kv_s), jnp.stack(kr_s))
```

```python
import functools
import math

import jax
import jax.numpy as jnp
from jax import lax
from jax.experimental import pallas as pl
from jax.experimental.pallas import tpu as pltpu

F32 = jnp.float32
BF16 = jnp.bfloat16

D_MODEL = 4096
BATCH = 4
SEQ = 2048
DEC_BATCH = 128
DEC_SEQ = 8
PAST_LEN = 16384
PAGE_SIZE = 128
N_PAGES = PAST_LEN // PAGE_SIZE
CHUNK = 128
A_GROUPS = 8
A_WIDTH = D_MODEL // 2
A_GROUP_DIM = A_WIDTH // A_GROUPS
B_WIDTH = D_MODEL // 2
AB_IN = 2 * A_WIDTH + 3 * B_WIDTH
N_HEADS = 32
QK_NOPE = 128
QK_ROPE = 64
V_HEAD = 128
Q_LORA = 1024
KV_LORA = 512
ROPE_THETA = 10000.0
ATTN_SCALE = 1.0 / math.sqrt(QK_NOPE + QK_ROPE)
D_FF = 11008
N_MOD = 6
EPS = 1e-6

TP = BATCH * SEQ
TS = DEC_BATCH * DEC_SEQ
T = TP + TS
N_SEQ = BATCH + DEC_BATCH
N_SEQ_PAD = 136

MIB = 1024 * 1024
W_CONVERT_ROWS = 512

ARB = "arbitrary"


def _params(n_axes, vmem_mib):
    return pltpu.CompilerParams(dimension_semantics=(ARB,) * n_axes,
                                vmem_limit_bytes=vmem_mib * MIB)


def _dot(a, b):
    return jnp.dot(a, b, preferred_element_type=F32)


def _dot_nt(a, b):
    return lax.dot_general(a, b, (((1,), (1,)), ((), ())), preferred_element_type=F32)


ADALN_TN = 512


def _adaln_body(c_ref, w_ref, b_ref, o_ref):
    c = c_ref[...]
    a = (c * jax.nn.sigmoid(c)).astype(BF16)
    w = w_ref[...].astype(BF16)
    o_ref[...] = _dot(a, w) + b_ref[...]


def _adaln(c_all, w_mod, b_mod):
    depth = w_mod.shape[0]
    n = w_mod.shape[2]
    return pl.pallas_call(
        _adaln_body,
        grid=(depth, n // ADALN_TN),
        in_specs=[
            pl.BlockSpec((N_SEQ_PAD, D_MODEL), lambda l, j: (0, 0)),
            pl.BlockSpec((None, D_MODEL, ADALN_TN), lambda l, j: (l, 0, j)),
            pl.BlockSpec((None, 1, ADALN_TN), lambda l, j: (l, 0, j)),
        ],
        out_specs=pl.BlockSpec((None, N_SEQ_PAD, ADALN_TN), lambda l, j: (l, 0, j)),
        out_shape=jax.ShapeDtypeStruct((depth, N_SEQ_PAD, n), F32),
        compiler_params=_params(2, 40),
        name="adaln",
    )(c_all, w_mod, b_mod.reshape(depth, 1, n))


def _mod_specs(tm, tn, col_blk, ij):
    n_p_tiles = TP // tm
    tiles_per_seq = SEQ // tm

    def p_map(*g):
        i, j = ij(*g)
        return (jnp.minimum(i // tiles_per_seq, BATCH - 1), 0, col_blk(j))

    def s_map(*g):
        i, j = ij(*g)
        return (jnp.maximum(i - n_p_tiles, 0), col_blk(j))

    return [pl.BlockSpec((None, 1, tn), p_map), pl.BlockSpec((tm, tn), s_map)]


def _pick_mod(i, tm, p_ref, s_ref):
    return jnp.where(i < TP // tm, p_ref[...], s_ref[...])


NORM_TM = 256


def _rms(x, g):
    return x * lax.rsqrt(jnp.mean(x * x, axis=-1, keepdims=True) + EPS) * g


def _norm_mod_body(x_ref, g_ref, shp_ref, shs_ref, scp_ref, scs_ref, o_ref):
    i = pl.program_id(0)
    y = _rms(x_ref[...], g_ref[...])

    @pl.when(i < TP // NORM_TM)
    def _():
        o_ref[...] = (y * (1.0 + scp_ref[...]) + shp_ref[...]).astype(BF16)

    @pl.when(i >= TP // NORM_TM)
    def _():
        o_ref[...] = (y * (1.0 + scs_ref[...]) + shs_ref[...]).astype(BF16)


def _norm_mod(x, g_all, g_idx, mod_p, mod_s, shift_chunk):
    ij = lambda i: (i, 0)
    specs = (_mod_specs(NORM_TM, D_MODEL, lambda j: shift_chunk, ij)
             + _mod_specs(NORM_TM, D_MODEL, lambda j: shift_chunk + 1, ij))
    return pl.pallas_call(
        _norm_mod_body,
        grid=(T // NORM_TM,),
        in_specs=[pl.BlockSpec((NORM_TM, D_MODEL), lambda i: (i, 0)),
                  pl.BlockSpec((None, 1, D_MODEL), lambda i: (g_idx, 0, 0))] + specs,
        out_specs=pl.BlockSpec((NORM_TM, D_MODEL), lambda i: (i, 0)),
        out_shape=jax.ShapeDtypeStruct((T, D_MODEL), BF16),
        compiler_params=_params(1, 48),
        name="norm_mod",
    )(x, g_all, mod_p, mod_s, mod_p, mod_s)


def _final_norm_body(x_ref, g_ref, o_ref):
    o_ref[...] = _rms(x_ref[...], g_ref[...])


def _final_norm(x, g, row_blk0, n_rows, out_shape3):
    s = out_shape3[1]
    if s >= NORM_TM:
        per = s // NORM_TM
        o_spec = pl.BlockSpec((None, NORM_TM, D_MODEL), lambda i: (i // per, i % per, 0))
        out = jax.ShapeDtypeStruct(out_shape3, F32)
    else:
        o_spec = pl.BlockSpec((NORM_TM, D_MODEL), lambda i: (i, 0))
        out = jax.ShapeDtypeStruct((n_rows, D_MODEL), F32)
    y = pl.pallas_call(
        _final_norm_body,
        grid=(n_rows // NORM_TM,),
        in_specs=[pl.BlockSpec((NORM_TM, D_MODEL), lambda i: (i + row_blk0, 0)),
                  pl.BlockSpec((1, D_MODEL), lambda i: (0, 0))],
        out_specs=o_spec,
        out_shape=out,
        compiler_params=_params(1, 32),
        name="final_norm",
    )(x, g.reshape(1, D_MODEL))
    return y.reshape(out_shape3)


def _mm(a, ws, epilogue, *, tm, tn, n_cols, out_shapes, out_specs, name, vmem_mib,
        w_layer=0, w_col_off=None, tk=None, kblk=0, extras=(), extra_specs=()):
    m = a.shape[0]
    tk = a.shape[1] if tk is None else tk
    nw, ne, no = len(ws), len(extras), len(out_shapes)
    w_col_off = (0,) * nw if w_col_off is None else w_col_off

    def body(*refs):
        a_ref = refs[0]
        w_refs = refs[1:1 + nw]
        ex = refs[1 + nw:1 + nw + ne]
        outs = refs[1 + nw + ne:1 + nw + ne + no]
        wbf = refs[1 + nw + ne + no:]

        @pl.when(pl.program_id(1) == 0)
        def _():
            for w, wb in zip(w_refs, wbf):
                for r in range(0, tk, W_CONVERT_ROWS):
                    rows = min(W_CONVERT_ROWS, tk - r)
                    wb[r:r + rows, :] = w[r:r + rows, :].astype(BF16)

        av = a_ref[...]
        accs = [_dot(av, wb[...]) for wb in wbf]
        epilogue(accs, ex, outs)

    in_specs = [pl.BlockSpec((tm, tk), lambda j, i: (i, kblk))]
    for off in w_col_off:
        in_specs.append(pl.BlockSpec((None, tk, tn), lambda j, i, off=off: (w_layer, kblk, j + off)))
    in_specs += list(extra_specs)
    return pl.pallas_call(
        body,
        grid=(n_cols // tn, m // tm),
        in_specs=in_specs,
        out_specs=out_specs,
        out_shape=out_shapes,
        scratch_shapes=[pltpu.VMEM((tk, tn), BF16) for _ in range(nw)],
        compiler_params=_params(2, vmem_mib),
        name=name,
    )(a, *ws, *extras)


def _tile_spec(tm, tn, col_off=0):
    return pl.BlockSpec((tm, tn), lambda j, i: (i, j + col_off))


def _residual_mm(a, w, w_layer, x, mod_p, mod_s, gate_chunk, *, tm, tn, name, vmem_mib,
                 tk=None, kblk=0, partial=None):
    ij = lambda j, i: (i, j)
    gate_specs = _mod_specs(tm, tn, lambda j: gate_chunk * (D_MODEL // tn) + j, ij)
    has_partial = partial is not None

    def epilogue(accs, ex, outs):
        i = pl.program_id(1)
        acc = accs[0]
        if has_partial:
            acc = acc + ex[3][...]
        gate = _pick_mod(i, tm, ex[1], ex[2])
        outs[0][...] = ex[0][...] + gate * acc

    extras = [x, mod_p, mod_s] + ([partial] if has_partial else [])
    extra_specs = [_tile_spec(tm, tn)] + gate_specs + ([_tile_spec(tm, tn)] if has_partial else [])
    return _mm(a, [w], epilogue, tm=tm, tn=tn, n_cols=D_MODEL,
               out_shapes=[jax.ShapeDtypeStruct((T, D_MODEL), F32)],
               out_specs=[_tile_spec(tm, tn)], name=name, vmem_mib=vmem_mib,
               w_layer=w_layer, tk=tk, kblk=kblk, extras=extras, extra_specs=extra_specs)[0]


def _partial_mm(a, w, w_layer, *, tm, tn, tk, kblk, name, vmem_mib):
    def epilogue(accs, ex, outs):
        outs[0][...] = accs[0]

    return _mm(a, [w], epilogue, tm=tm, tn=tn, n_cols=D_MODEL,
               out_shapes=[jax.ShapeDtypeStruct((T, D_MODEL), F32)],
               out_specs=[_tile_spec(tm, tn)], name=name, vmem_mib=vmem_mib,
               w_layer=w_layer, tk=tk, kblk=kblk)[0]


MIX_TM = CHUNK


def _layer_norm(x, g):
    mu = jnp.mean(x, axis=-1, keepdims=True)
    xc = x - mu
    return xc * lax.rsqrt(jnp.mean(xc * xc, axis=-1, keepdims=True) + EPS) * g


def _mixer_body(is_prompt, *refs):
    if not is_prompt:
        _mixer_tile(False, *refs)
        return
    i = pl.program_id(0)

    @pl.when(i < TP // MIX_TM)
    def _():
        _mixer_tile(True, *refs)

    @pl.when(i >= TP // MIX_TM)
    def _():
        yab_ref = refs[-2]
        yab_ref[...] = jnp.zeros_like(yab_ref)


def _mixer_tile(is_prompt, u_ref, vg_ref, gb_ref, gc_ref, hi_ref, h1_ref, h2_ref,
                ws_ref, bs_ref, gv_ref, cw_ref, *rest):
    if is_prompt:
        yab_ref, tail_ref = rest
    else:
        _, yab_ref, tail_ref, v_ref = rest
    i = pl.program_id(0)
    v = _layer_norm(vg_ref[...], gv_ref[...])
    if not is_prompt:
        v_ref[...] = v
    vb = v.astype(BF16)
    row = lax.broadcasted_iota(jnp.int32, (MIX_TM, MIX_TM), 0)
    col = lax.broadcasted_iota(jnp.int32, (MIX_TM, MIX_TM), 1)
    mask = col <= row
    if not is_prompt:
        mask = mask & ((col // DEC_SEQ) == (row // DEC_SEQ))
    for g in range(A_GROUPS):
        sl = slice(g * A_GROUP_DIM, (g + 1) * A_GROUP_DIM)
        w = jnp.where(mask, ws_ref[g], 0.0).astype(BF16)
        s = _dot(w, vb[:, sl]) + bs_ref[g]
        yab_ref[:, sl] = (u_ref[:, sl] * s).astype(BF16)

    gch = gc_ref[...] * hi_ref[...]
    pos = lax.broadcasted_iota(jnp.int32, (MIX_TM, B_WIDTH), 0)
    r1 = pltpu.roll(gch, 1, 0)
    r2 = pltpu.roll(gch, 2, 0)
    if is_prompt:
        halo = jnp.where(i % (SEQ // MIX_TM) == 0, 0.0, h1_ref[...] * h2_ref[...])
        p1 = halo[7:8, :]
        p2 = halo[6:7, :]
        x1 = jnp.where(pos == 0, p1, r1)
        x2 = jnp.where(pos == 0, p2, jnp.where(pos == 1, p1, r2))
    else:
        pos = pos % DEC_SEQ
        e0 = h1_ref[...]
        e1 = h2_ref[...]
        x1 = jnp.where(pos == 0, e1, r1)
        x2 = jnp.where(pos == 0, e0, jnp.where(pos == 1, e1, r2))
    yc = x2 * cw_ref[0:1, :] + x1 * cw_ref[1:2, :] + gch * cw_ref[2:3, :]
    yab_ref[:, A_WIDTH:] = (gb_ref[...] * yc).astype(BF16)
    if is_prompt:
        tail_ref[...] = gch[MIX_TM - 8:, :]
    else:
        tail_ref[...] = gch


def _mixer(z, is_prompt, halo_a, halo_b, ws_eff, bs_eff, g_v, conv_w, yab_in=None):
    n_tiles = (TP if is_prompt else TS) // MIX_TM
    blk0 = 0 if is_prompt else TP // MIX_TM
    tile = (lambda i: jnp.minimum(i, n_tiles - 1)) if is_prompt else (lambda i: i)
    zc = lambda c: pl.BlockSpec((MIX_TM, A_WIDTH), lambda i, c=c: (tile(i) + blk0, c))
    if is_prompt:
        rows8 = MIX_TM // 8
        halo_specs = [pl.BlockSpec((8, B_WIDTH), lambda i: (jnp.maximum(tile(i) * rows8 - 1, 0), 3)),
                      pl.BlockSpec((8, B_WIDTH), lambda i: (jnp.maximum(tile(i) * rows8 - 1, 0), 4))]
    else:
        halo_specs = [pl.BlockSpec((MIX_TM, B_WIDTH), lambda i: (i, 0)),
                      pl.BlockSpec((MIX_TM, B_WIDTH), lambda i: (i, 0))]
    in_specs = [zc(0), zc(1), zc(2), zc(3), zc(4)] + halo_specs + [
        pl.BlockSpec((A_GROUPS, MIX_TM, MIX_TM), lambda i: (0, 0, 0)),
        pl.BlockSpec((A_GROUPS, MIX_TM, 1), lambda i: (0, 0, 0)),
        pl.BlockSpec((1, A_WIDTH), lambda i: (0, 0)),
        pl.BlockSpec((3, B_WIDTH), lambda i: (0, 0)),
    ]
    args = [z, z, z, z, z, halo_a, halo_b, ws_eff, bs_eff, g_v, conv_w]
    yab_spec = pl.BlockSpec((MIX_TM, D_MODEL), lambda i: (i + blk0, 0))
    yab_shape = jax.ShapeDtypeStruct((T, D_MODEL), BF16)
    if is_prompt:
        out_shapes = [yab_shape, jax.ShapeDtypeStruct((n_tiles * 8, B_WIDTH), F32)]
        out_specs = [yab_spec, pl.BlockSpec((8, B_WIDTH), lambda i: (tile(i), 0))]
        aliases = {}
    else:
        in_specs.append(pl.BlockSpec(memory_space=pl.ANY))
        args.append(yab_in)
        out_shapes = [yab_shape, jax.ShapeDtypeStruct((TS, B_WIDTH), F32),
                      jax.ShapeDtypeStruct((TS, A_WIDTH), F32)]
        out_specs = [yab_spec, pl.BlockSpec((MIX_TM, B_WIDTH), lambda i: (i, 0)),
                     pl.BlockSpec((MIX_TM, A_WIDTH), lambda i: (i, 0))]
        aliases = {len(args) - 1: 0}
    return pl.pallas_call(
        functools.partial(_mixer_body, is_prompt),
        grid=(T // MIX_TM if is_prompt else n_tiles,),
        in_specs=in_specs,
        out_specs=out_specs,
        out_shape=out_shapes,
        input_output_aliases=aliases,
        compiler_params=_params(1, 40),
        name="mixer_prompt" if is_prompt else "mixer_sample",
    )(*args)


Q_TM = 1024
Q_HEADS = 4
Q_SAMPLE_TILE = TP // Q_TM


def _q_lat_body(cq_ref, wq_ref, wuk_ref, o_ref, os_ref, wq_bf, wuk_bf):
    i = pl.program_id(1)

    @pl.when(i == 0)
    def _():
        wq_bf[...] = wq_ref[...].astype(BF16)
        wuk_bf[...] = wuk_ref[...].astype(BF16)

    qn = _dot(cq_ref[...], wq_bf[...]).astype(BF16)
    for hh in range(Q_HEADS):
        ql = _dot(qn[:, hh * QK_NOPE:(hh + 1) * QK_NOPE], wuk_bf[hh])
        o_ref[:, hh * KV_LORA:(hh + 1) * KV_LORA] = ql.astype(BF16)

        @pl.when(i == Q_SAMPLE_TILE)
        def _():
            os_ref[:, hh * KV_LORA:(hh + 1) * KV_LORA] = ql


def _q_lat(cq, w_q_nope, w_uk_t):
    tn_in = Q_HEADS * QK_NOPE
    tn_out = Q_HEADS * KV_LORA
    return pl.pallas_call(
        _q_lat_body,
        grid=(N_HEADS // Q_HEADS, T // Q_TM),
        in_specs=[pl.BlockSpec((Q_TM, Q_LORA), lambda j, i: (i, 0)),
                  pl.BlockSpec((Q_LORA, tn_in), lambda j, i: (0, j)),
                  pl.BlockSpec((Q_HEADS, QK_NOPE, KV_LORA), lambda j, i: (j, 0, 0))],
        out_specs=[pl.BlockSpec((Q_TM, tn_out), lambda j, i: (i, j)),
                   pl.BlockSpec((TS, tn_out), lambda j, i: (0, j))],
        out_shape=[jax.ShapeDtypeStruct((T, N_HEADS * KV_LORA), BF16),
                   jax.ShapeDtypeStruct((TS, N_HEADS * KV_LORA), F32)],
        scratch_shapes=[pltpu.VMEM((Q_LORA, tn_in), BF16),
                        pltpu.VMEM((Q_HEADS, QK_NOPE, KV_LORA), BF16)],
        compiler_params=_params(2, 48),
        name="mla_q_lat",
    )(cq, w_q_nope, w_uk_t)


def _uv_sample_body(a_ref, w_ref, _, o_ref):
    o_ref[...] = _dot(a_ref[...].astype(BF16), w_ref[...]).astype(BF16)


def _uv_sample(o_lat_s, w_uv_bf, o_in):
    return pl.pallas_call(
        _uv_sample_body,
        grid=(N_HEADS,),
        in_specs=[pl.BlockSpec((TS, KV_LORA), lambda h: (0, h)),
                  pl.BlockSpec((None, KV_LORA, V_HEAD), lambda h: (h, 0, 0)),
                  pl.BlockSpec(memory_space=pl.ANY)],
        out_specs=pl.BlockSpec((TS, V_HEAD), lambda h: (TP // TS, h)),
        out_shape=jax.ShapeDtypeStruct((T, N_HEADS * V_HEAD), BF16),
        input_output_aliases={2: 0},
        compiler_params=_params(1, 32),
        name="mla_uv_sample",
    )(o_lat_s, w_uv_bf, o_in)


PQ = 128
PK = 256
P_ROWS = PQ * N_HEADS
PR = 512
PR_HEADS = PR // PQ


def _prompt_attend(nk, qb, qs, qrs, kv_ref, kr_ref, wuv_ref, o3):
    def scores(c):
        r0 = pl.multiple_of(c * PR, PR)
        return _dot_nt(qs[pl.ds(r0, PR), :], kv_ref[0:nk, :]) + _dot_nt(qrs[pl.ds(r0, PR), :], kr_ref[0:nk, :])

    def finish(c, s):
        kv = kv_ref[0:nk, :]
        tok = qb * PQ + lax.broadcasted_iota(jnp.int32, (PR, PK), 0) % PQ
        key = (nk - PK) + lax.broadcasted_iota(jnp.int32, (PR, PK), 1)
        tail = jnp.where(key <= tok, s[:, nk - PK:], -jnp.inf)
        m = jnp.max(tail, axis=-1, keepdims=True)
        if nk > PK:
            head = s[:, :nk - PK]
            m = jnp.maximum(m, jnp.max(head, axis=-1, keepdims=True))
            p_head = jnp.exp((head - m) * ATTN_SCALE)
            p_tail = jnp.exp((tail - m) * ATTN_SCALE)
            l = jnp.sum(p_head, axis=-1, keepdims=True) + jnp.sum(p_tail, axis=-1, keepdims=True)
            p = jnp.concatenate([p_head.astype(BF16), p_tail.astype(BF16)], axis=1)
        else:
            p_tail = jnp.exp((tail - m) * ATTN_SCALE)
            l = jnp.sum(p_tail, axis=-1, keepdims=True)
            p = p_tail.astype(BF16)
        o_lat = (_dot(p, kv) * (1.0 / l)).astype(BF16)
        for hh in range(PR_HEADS):
            h = c * PR_HEADS + hh
            o3[h] = _dot(o_lat[hh * PQ:(hh + 1) * PQ, :], wuv_ref[h]).astype(BF16)

    def pair(cp, carry):
        s_a = scores(2 * cp)
        s_b = scores(2 * cp + 1)
        finish(2 * cp, s_a)
        finish(2 * cp + 1, s_b)
        return carry

    lax.fori_loop(0, P_ROWS // PR // 2, pair, 0)


def _prompt_attn_body(ql_ref, qr_ref, kv_ref, kr_ref, wuv_ref, o_ref, qs, qrs, o3):
    t = pl.program_id(0)
    qb = t % (SEQ // PQ)
    is_prompt = t < TP // PQ

    @pl.when(is_prompt)
    def _():
        for h in range(N_HEADS):
            qs[h * PQ:(h + 1) * PQ, :] = ql_ref[:, h * KV_LORA:(h + 1) * KV_LORA]
            qrs[h * PQ:(h + 1) * PQ, :] = qr_ref[:, h * QK_ROPE:(h + 1) * QK_ROPE]

    for v in range(SEQ // PK):
        @pl.when(is_prompt & ((qb * PQ) // PK == v))
        def _():
            _prompt_attend((v + 1) * PK, qb, qs, qrs, kv_ref, kr_ref, wuv_ref, o3)

    @pl.when(is_prompt)
    def _():
        for h in range(N_HEADS):
            o_ref[:, h * V_HEAD:(h + 1) * V_HEAD] = o3[h]

    @pl.when(jnp.logical_not(is_prompt))
    def _():
        o_ref[...] = jnp.zeros_like(o_ref)


def _prompt_attn(q_lat, q_rope, ckv_bf, kr_bf, w_uv_bf):
    nq = SEQ // PQ
    last = TP // PQ - 1
    q_map = lambda t: (jnp.minimum(t, last), 0)
    k_map = lambda t: (jnp.minimum(t, last) // nq, 0)
    return pl.pallas_call(
        _prompt_attn_body,
        grid=(T // PQ,),
        in_specs=[pl.BlockSpec((PQ, N_HEADS * KV_LORA), q_map),
                  pl.BlockSpec((PQ, N_HEADS * QK_ROPE), q_map),
                  pl.BlockSpec((SEQ, KV_LORA), k_map),
                  pl.BlockSpec((SEQ, QK_ROPE), k_map),
                  pl.BlockSpec((N_HEADS, KV_LORA, V_HEAD), lambda t: (0, 0, 0))],
        out_specs=pl.BlockSpec((PQ, N_HEADS * V_HEAD), lambda t: (t, 0)),
        out_shape=jax.ShapeDtypeStruct((T, N_HEADS * V_HEAD), BF16),
        scratch_shapes=[pltpu.VMEM((P_ROWS, KV_LORA), BF16), pltpu.VMEM((P_ROWS, QK_ROPE), BF16),
                        pltpu.VMEM((N_HEADS, PQ, V_HEAD), BF16)],
        compiler_params=_params(1, 56),
        name="prompt_attn",
    )(q_lat, q_rope, ckv_bf, kr_bf, w_uv_bf)


S_PAGES = 32
S_CHUNKS = N_PAGES // S_PAGES
S_ROWS = DEC_SEQ * N_HEADS
S_SLOTS = 3


S_SUB = 8
S_SUBS = S_PAGES // S_SUB


def _online_softmax(s, v, m, l, acc):
    m_new = jnp.maximum(m, jnp.max(s, axis=-1, keepdims=True))
    alpha = jnp.exp((m - m_new) * ATTN_SCALE)
    p = jnp.exp((s - m_new) * ATTN_SCALE)
    l = alpha * l + jnp.sum(p, axis=-1, keepdims=True)
    acc = alpha * acc + _dot(p.astype(BF16), v)
    return m_new, l, acc


def _page_copies(layer, pt_ref, cache_kv, cache_kr_t, kvbuf, krbuf, sem, step, slot, known_pages):
    copies = []
    for k in range(S_PAGES):
        page = pt_ref[step * S_PAGES + k] if known_pages else 0
        copies.append(pltpu.make_async_copy(cache_kv.at[layer, page], kvbuf.at[slot, k], sem.at[slot]))
        copies.append(pltpu.make_async_copy(cache_kr_t.at[layer, page], krbuf.at[slot, k], sem.at[slot]))
    return copies


def _sample_attn_body(layer, pt_ref, ql_ref, qr_ref, kvn_ref, krn_ref, cache_kv, cache_kr_t, o_ref,
                      kvbuf, krbuf, sem, qs, qrs, m_ref, l_ref, acc_ref):
    c = pl.program_id(1)
    step = pl.program_id(0) * S_CHUNKS + c
    slot = lax.rem(step, S_SLOTS)
    copies = functools.partial(_page_copies, layer, pt_ref, cache_kv, cache_kr_t, kvbuf, krbuf, sem)

    @pl.when(step == 0)
    def _():
        for ahead in range(S_SLOTS - 1):
            for cp in copies(ahead, ahead, True):
                cp.start()

    for cp in copies(step, slot, False):
        cp.wait()

    @pl.when(c == 0)
    def _():
        for h in range(N_HEADS):
            qs[h * DEC_SEQ:(h + 1) * DEC_SEQ, :] = ql_ref[:, h * KV_LORA:(h + 1) * KV_LORA]
            qrs[h * DEC_SEQ:(h + 1) * DEC_SEQ, :] = qr_ref[:, h * QK_ROPE:(h + 1) * QK_ROPE]
        m_ref[...] = jnp.full_like(m_ref, -jnp.inf)
        l_ref[...] = jnp.zeros_like(l_ref)
        acc_ref[...] = jnp.zeros_like(acc_ref)

    q = qs[...].astype(BF16)
    qr = qrs[...].astype(BF16)
    m, l, acc = m_ref[...], l_ref[...], acc_ref[...]
    def scores(sb):
        pages = range(sb * S_SUB, (sb + 1) * S_SUB)
        kv = jnp.concatenate([kvbuf[slot, k].astype(BF16) for k in pages], axis=0)
        kr_t = jnp.concatenate([krbuf[slot, k].astype(BF16) for k in pages], axis=1)
        return _dot_nt(q, kv) + _dot(qr, kr_t), kv

    pending = scores(0)
    for sb in range(S_SUBS):
        following = scores(sb + 1) if sb + 1 < S_SUBS else None
        m, l, acc = _online_softmax(*pending, m, l, acc)
        pending = following
    m_ref[...] = m
    l_ref[...] = l
    acc_ref[...] = acc

    @pl.when(c == S_CHUNKS - 1)
    def _():
        pad = PAGE_SIZE - DEC_SEQ
        kvn = jnp.concatenate([kvn_ref[...], jnp.zeros((pad, KV_LORA), F32)], axis=0).astype(BF16)
        krn = jnp.concatenate([krn_ref[...], jnp.zeros((pad, QK_ROPE), F32)], axis=0).astype(BF16)
        sn = _dot_nt(q, kvn) + _dot_nt(qr, krn)
        tok = lax.broadcasted_iota(jnp.int32, (S_ROWS, PAGE_SIZE), 0) % DEC_SEQ
        key = lax.broadcasted_iota(jnp.int32, (S_ROWS, PAGE_SIZE), 1)
        sn = jnp.where(key <= tok, sn, -jnp.inf)
        _, l2, acc2 = _online_softmax(sn, kvn, m, l, acc)
        o = acc2 * (1.0 / l2)
        for h in range(N_HEADS):
            o_ref[:, h * KV_LORA:(h + 1) * KV_LORA] = o[h * DEC_SEQ:(h + 1) * DEC_SEQ, :]

    nxt = step + (S_SLOTS - 1)

    @pl.when(nxt < DEC_BATCH * S_CHUNKS)
    def _():
        for n, cp in enumerate(copies(nxt, lax.rem(nxt, S_SLOTS), True)):
            cp.start(priority=(n // 2) % 2)


def _sample_attn(page_table, q_lat_s, q_rope_s, cache_kv, cache_kr_t, layer, ckv_f32, kr_f32):
    q_map = lambda b, c, pt: (b, 0)
    new_map = lambda b, c, pt: (TP // DEC_SEQ + b, 0)
    grid_spec = pltpu.PrefetchScalarGridSpec(
        num_scalar_prefetch=1,
        grid=(DEC_BATCH, S_CHUNKS),
        in_specs=[pl.BlockSpec((DEC_SEQ, N_HEADS * KV_LORA), q_map),
                  pl.BlockSpec((DEC_SEQ, N_HEADS * QK_ROPE), q_map),
                  pl.BlockSpec((DEC_SEQ, KV_LORA), new_map),
                  pl.BlockSpec((DEC_SEQ, QK_ROPE), new_map),
                  pl.BlockSpec(memory_space=pl.ANY),
                  pl.BlockSpec(memory_space=pl.ANY)],
        out_specs=pl.BlockSpec((DEC_SEQ, N_HEADS * KV_LORA), q_map),
        scratch_shapes=[pltpu.VMEM((S_SLOTS, S_PAGES, PAGE_SIZE, KV_LORA), F32),
                        pltpu.VMEM((S_SLOTS, S_PAGES, QK_ROPE, PAGE_SIZE), F32),
                        pltpu.SemaphoreType.DMA((S_SLOTS,)),
                        pltpu.VMEM((S_ROWS, KV_LORA), F32), pltpu.VMEM((S_ROWS, QK_ROPE), F32),
                        pltpu.VMEM((S_ROWS, 1), F32), pltpu.VMEM((S_ROWS, 1), F32),
                        pltpu.VMEM((S_ROWS, KV_LORA), F32)],
    )
    return pl.pallas_call(
        functools.partial(_sample_attn_body, layer),
        grid_spec=grid_spec,
        out_shape=jax.ShapeDtypeStruct((TS, N_HEADS * KV_LORA), F32),
        compiler_params=_params(2, 48),
        name="sample_attn",
    )(page_table.reshape(-1), q_lat_s, q_rope_s, ckv_f32, kr_f32, cache_kv, cache_kr_t)


def _ffn(x, h, li, w_gate, w_up, w_down, mod_p, mod_s):
    def ep_glu(accs, ex, outs):
        g, u = accs
        outs[0][...] = (g * jax.nn.sigmoid(g) * u).astype(BF16)

    tn = 256
    act = _mm(h, [w_gate, w_up], ep_glu, tm=1536, tn=tn, n_cols=D_FF,
              out_shapes=[jax.ShapeDtypeStruct((T, D_FF), BF16)],
              out_specs=[_tile_spec(1536, tn)], name="ffn_gate_up", vmem_mib=56, w_layer=li)[0]
    half = D_FF // 2
    part = _partial_mm(act, w_down, li, tm=512, tn=512, tk=half, kblk=0, name="ffn_down_a", vmem_mib=48)
    return _residual_mm(act, w_down, li, x, mod_p, mod_s, 5, tm=512, tn=512, tk=half, kblk=1,
                        partial=part, name="ffn_down_b", vmem_mib=52)


def _ab_layer(x, h, a, state_conv, w_ab_in, w_ab_out, a_spatial_w, a_spatial_b, a_v_norm_g, b_conv_w,
              mod_p, mod_s):
    def ep_in(accs, ex, outs):
        j = pl.program_id(0)
        z = accs[0]

        @pl.when(j < (2 * A_WIDTH) // 512)
        def _():
            outs[0][...] = jax.nn.gelu(z)

        @pl.when(j >= (2 * A_WIDTH) // 512)
        def _():
            outs[0][...] = z

    z = _mm(h, [w_ab_in], ep_in, tm=1024, tn=512, n_cols=AB_IN,
            out_shapes=[jax.ShapeDtypeStruct((T, AB_IN), F32)],
            out_specs=[_tile_spec(1024, 512)], name="ab_in", vmem_mib=48, w_layer=a)[0]

    w_s = a_spatial_w[a]
    b_s = a_spatial_b[a]
    reps = MIX_TM // DEC_SEQ
    ws_s = jnp.tile(w_s[:, :DEC_SEQ, :DEC_SEQ], (1, reps, reps))
    bs_s = jnp.tile(b_s[:, :DEC_SEQ], (1, reps))
    g_v = a_v_norm_g[a].reshape(1, A_WIDTH)
    conv_w = b_conv_w[a]
    e0 = jnp.repeat(state_conv[a, :, 0, :], DEC_SEQ, axis=0)
    e1 = jnp.repeat(state_conv[a, :, 1, :], DEC_SEQ, axis=0)

    yab, tail_p = _mixer(z, True, z, z, w_s, b_s[:, :, None], g_v, conv_w)
    yab, gch_s, v_s = _mixer(z, False, e0, e1, ws_s, bs_s[:, :, None], g_v, conv_w, yab_in=yab)

    x = _residual_mm(yab, w_ab_out, a, x, mod_p, mod_s, 2, tm=1024, tn=512, name="ab_out", vmem_mib=56)
    conv_p = tail_p.reshape(BATCH, SEQ // MIX_TM, 8, B_WIDTH)[:, -1, 8 - 2:, :]
    conv_s = gch_s.reshape(DEC_BATCH, DEC_SEQ, B_WIDTH)[:, DEC_SEQ - 2:, :]
    return x, conv_p, conv_s, v_s.reshape(DEC_BATCH, DEC_SEQ, A_WIDTH)


def _rope_tables():
    half = QK_ROPE // 2
    inv = ROPE_THETA ** (-jnp.arange(half, dtype=F32) / half)
    pos_p = jnp.arange(TP, dtype=jnp.int32) % SEQ
    pos_s = PAST_LEN + jnp.arange(TS, dtype=jnp.int32) % DEC_SEQ
    pos = jnp.concatenate([pos_p, pos_s])
    ang = pos.astype(F32)[:, None] * inv[None, :]
    cos, sin = jnp.cos(ang), jnp.sin(ang)
    return jnp.concatenate([cos, cos], axis=-1), jnp.concatenate([-sin, sin], axis=-1)


def _swap_halves(w, n_groups):
    k = w.shape[0]
    w3 = w.reshape(k, n_groups, QK_ROPE)
    half = QK_ROPE // 2
    return jnp.concatenate([w3[..., half:], w3[..., :half]], axis=-1).reshape(k, n_groups * QK_ROPE)


def _mla_layer(x, h, m, cache_kv, cache_kr, page_table, c_w_in, c_q_norm_g, c_kv_norm_g, c_w_q_up,
               c_w_kv_up, c_w_out, mod_p, mod_s, cos64, sin64):
    def ep_cq(accs, ex, outs):
        outs[0][...] = _rms(accs[0], ex[0][...]).astype(BF16)

    cq = _mm(h, [c_w_in], ep_cq, tm=512, tn=Q_LORA, n_cols=Q_LORA,
             out_shapes=[jax.ShapeDtypeStruct((T, Q_LORA), BF16)],
             out_specs=[_tile_spec(512, Q_LORA)], name="mla_in_q", vmem_mib=56, w_layer=m,
             extras=[c_q_norm_g[m].reshape(1, Q_LORA)],
             extra_specs=[pl.BlockSpec((1, Q_LORA), lambda j, i: (0, 0))])[0]

    def ep_ckv(accs, ex, outs):
        y = _rms(accs[0], ex[0][...])
        outs[0][...] = y
        outs[1][...] = y.astype(BF16)

    ckv, ckv_bf = _mm(h, [c_w_in], ep_ckv, tm=1024, tn=KV_LORA, n_cols=KV_LORA,
                      out_shapes=[jax.ShapeDtypeStruct((T, KV_LORA), F32),
                                  jax.ShapeDtypeStruct((T, KV_LORA), BF16)],
                      out_specs=[_tile_spec(1024, KV_LORA)] * 2, name="mla_in_kv", vmem_mib=48,
                      w_layer=m, w_col_off=(Q_LORA // KV_LORA,),
                      extras=[c_kv_norm_g[m].reshape(1, KV_LORA)],
                      extra_specs=[pl.BlockSpec((1, KV_LORA), lambda j, i: (0, 0))])

    def ep_rope(accs, ex, outs):
        y = accs[0] * ex[0][...] + accs[1] * ex[1][...]
        for o in outs:
            o[...] = y.astype(o.dtype)

    w_kr = c_w_in[m, :, Q_LORA + KV_LORA:]
    w_kr_sw = _swap_halves(w_kr, 1)
    kr, kr_bf = _mm(h, [w_kr[None], w_kr_sw[None]], ep_rope, tm=1024, tn=QK_ROPE, n_cols=QK_ROPE,
                    out_shapes=[jax.ShapeDtypeStruct((T, QK_ROPE), F32),
                                jax.ShapeDtypeStruct((T, QK_ROPE), BF16)],
                    out_specs=[_tile_spec(1024, QK_ROPE)] * 2, name="mla_in_kr", vmem_mib=40,
                    extras=[cos64, sin64],
                    extra_specs=[pl.BlockSpec((1024, QK_ROPE), lambda j, i: (i, 0))] * 2)

    w_q3 = c_w_q_up[m].reshape(Q_LORA, N_HEADS, QK_NOPE + QK_ROPE)
    w_q_nope = w_q3[:, :, :QK_NOPE].reshape(Q_LORA, N_HEADS * QK_NOPE)
    w_q_rope = w_q3[:, :, QK_NOPE:].reshape(Q_LORA, N_HEADS * QK_ROPE)
    w_q_rope_sw = _swap_halves(w_q_rope, N_HEADS)

    def ep_q_rope(accs, ex, outs):
        y = accs[0] * ex[0][...] + accs[1] * ex[1][...]
        outs[0][...] = y.astype(BF16)

        @pl.when(pl.program_id(1) == TP // 1024)
        def _():
            outs[1][...] = y

    reps = 512 // QK_ROPE
    cos512 = jnp.tile(cos64, (1, reps))
    sin512 = jnp.tile(sin64, (1, reps))
    q_rope, q_rope_s = _mm(cq, [w_q_rope[None], w_q_rope_sw[None]], ep_q_rope, tm=1024, tn=512,
                           n_cols=N_HEADS * QK_ROPE,
                           out_shapes=[jax.ShapeDtypeStruct((T, N_HEADS * QK_ROPE), BF16),
                                       jax.ShapeDtypeStruct((TS, N_HEADS * QK_ROPE), F32)],
                           out_specs=[_tile_spec(1024, 512),
                                      pl.BlockSpec((TS, 512), lambda j, i: (0, j))],
                           name="mla_q_rope", vmem_mib=40, extras=[cos512, sin512],
                           extra_specs=[pl.BlockSpec((1024, 512), lambda j, i: (i, 0))] * 2)

    w_kv3 = c_w_kv_up[m].reshape(KV_LORA, N_HEADS, QK_NOPE + V_HEAD)
    w_uk_t = jnp.transpose(w_kv3[:, :, :QK_NOPE], (1, 2, 0))
    w_uv_bf = jnp.transpose(w_kv3[:, :, QK_NOPE:], (1, 0, 2)).astype(BF16)
    q_lat, q_lat_s = _q_lat(cq, w_q_nope, w_uk_t)

    o = _prompt_attn(q_lat, q_rope, ckv_bf, kr_bf, w_uv_bf)
    cache_kr_t = jnp.swapaxes(cache_kr, 2, 3)
    o_lat_s = _sample_attn(page_table, q_lat_s, q_rope_s, cache_kv, cache_kr_t, m, ckv, kr)
    o = _uv_sample(o_lat_s, w_uv_bf, o)
    x = _residual_mm(o, c_w_out, m, x, mod_p, mod_s, 2, tm=1024, tn=512, name="mla_out", vmem_mib=56)
    return x, ckv, kr


def kernel(x_prompt, x_sample, state_conv, cache_kv_latent, cache_k_rope, page_table, c_prompt, c_sample,
           norm_g, w_mod, b_mod, w_ab_in, w_ab_out, a_spatial_w, a_spatial_b, a_v_norm_g, b_conv_w,
           c_w_in, c_q_norm_g, c_kv_norm_g, c_w_q_up, c_w_kv_up, c_w_out,
           ffn_w_gate, ffn_w_up, ffn_w_down, final_norm_g):
    depth = norm_g.shape[0]
    x = jnp.concatenate([x_prompt.reshape(TP, D_MODEL), x_sample.reshape(TS, D_MODEL)], axis=0)
    c_all = jnp.concatenate([c_prompt, c_sample, jnp.zeros((N_SEQ_PAD - N_SEQ, D_MODEL), F32)], axis=0)
    mod = _adaln(c_all, w_mod, b_mod)
    g_all = norm_g.reshape(depth * 2, 1, D_MODEL)
    cos64, sin64 = _rope_tables()

    conv_p, conv_s, chunk_v_s = [], [], []
    kv_p, kr_p, kv_s, kr_s = [], [], [], []
    for li in range(depth):
        mod_p = mod[li, :BATCH].reshape(BATCH, 1, N_MOD * D_MODEL)
        mod_s = jnp.repeat(mod[li, BATCH:N_SEQ], DEC_SEQ, axis=0)
        h = _norm_mod(x, g_all, li * 2, mod_p, mod_s, 0)
        if li % 2 == 0:
            a = li // 2
            x, cp, cs, vs = _ab_layer(x, h, a, state_conv, w_ab_in, w_ab_out, a_spatial_w, a_spatial_b,
                                      a_v_norm_g, b_conv_w, mod_p, mod_s)
            conv_p.append(cp)
            conv_s.append(cs)
            chunk_v_s.append(vs)
        else:
            m = li // 2
            x, ckv, kr = _mla_layer(x, h, m, cache_kv_latent, cache_k_rope, page_table, c_w_in, c_q_norm_g,
                                    c_kv_norm_g, c_w_q_up, c_w_kv_up, c_w_out, mod_p, mod_s, cos64, sin64)
            kv_p.append(ckv[:TP].reshape(BATCH, SEQ, KV_LORA))
            kr_p.append(kr[:TP].reshape(BATCH, SEQ, QK_ROPE))
            kv_s.append(ckv[TP:].reshape(DEC_BATCH, DEC_SEQ, KV_LORA))
            kr_s.append(kr[TP:].reshape(DEC_BATCH, DEC_SEQ, QK_ROPE))
        h = _norm_mod(x, g_all, li * 2 + 1, mod_p, mod_s, 3)
        x = _ffn(x, h, li, ffn_w_gate, ffn_w_up, ffn_w_down, mod_p, mod_s)

    y_prompt = _final_norm(x, final_norm_g, 0, TP, (BATCH, SEQ, D_MODEL))
    y_sample = _final_norm(x, final_norm_g, TP // NORM_TM, TS, (DEC_BATCH, DEC_SEQ, D_MODEL))
    return (y_prompt, y_sample, jnp.stack(conv_p), jnp.stack(conv_s), jnp.stack(chunk_v_s),
            jnp.stack(kv_p), jnp.stack(kr_p), jnp.stack(kv_s), jnp.stack(kr_s))
```
